```python
import jax, jax.numpy as jnp
from jax import lax
import numpy as np

D_MODEL = 1024
BATCH = 32
SEQ = 2048
DEPTH = 1

GRID_W = 64
CTX_LEN = 256
HEAD_DIM = 64
N_Q_HEADS = (D_MODEL // 2) // HEAD_DIM
N_KV_HEADS = max(1, N_Q_HEADS // 4)
GQA_GROUP = N_Q_HEADS // N_KV_HEADS
ATTN_WIDTH = N_Q_HEADS * HEAD_DIM
KV_WIDTH = N_KV_HEADS * HEAD_DIM
CONV_WIDTH = D_MODEL - ATTN_WIDTH
CONV_SIZE = 31
WINDOW = 128
BLOCK = 128
ROPE_BASE = 10000.0
ROT_AXIS_DIM = HEAD_DIM // 2
LN_EPS = 1e-6
NEG_INF = -1e30
ALPHA = (2.0 * DEPTH) ** 0.25
BETA = (8.0 * DEPTH) ** -0.25

Q_OFF = 0
K_OFF = Q_OFF + ATTN_WIDTH
V_OFF = K_OFF + KV_WIDTH
AG_OFF = V_OFF + KV_WIDTH
CA_OFF = AG_OFF + ATTN_WIDTH
CB_OFF = CA_OFF + CONV_WIDTH
CG_OFF = CB_OFF + CONV_WIDTH
IN_WIDTH = CG_OFF + CONV_WIDTH

kernel_name = "hymba_attn_conformer_deepnorm_prefix"


def _norm(x):
    xf = x.astype(jnp.float32)
    mu = jnp.mean(xf, axis=-1, keepdims=True)
    var = jnp.mean(jnp.square(xf - mu), axis=-1, keepdims=True)
    return ((xf - mu) * lax.rsqrt(var + LN_EPS)).astype(x.dtype)


def _layer_norm(x, g, b):
    return _norm(x) * g + b


def _adaln(cond, w_ada, b_ada):
    mod = jax.nn.silu(cond) @ w_ada + b_ada
    return jnp.split(mod, 3, axis=-1)


def _axial_angles(n):
    rows = n // GRID_W
    row = jnp.repeat(jnp.arange(rows, dtype=jnp.float32), GRID_W)
    col = jnp.tile(jnp.arange(GRID_W, dtype=jnp.float32), rows)
    inv_freq = ROPE_BASE ** (-jnp.arange(0, ROT_AXIS_DIM, 2, dtype=jnp.float32) / ROT_AXIS_DIM)
    return row[:, None] * inv_freq, col[:, None] * inv_freq


def _rope_half(x, ang):
    x1, x2 = jnp.split(x, 2, axis=-1)
    cos = jnp.cos(ang)[None, :, None, :].astype(x.dtype)
    sin = jnp.sin(ang)[None, :, None, :].astype(x.dtype)
    return jnp.concatenate([x1 * cos - x2 * sin, x2 * cos + x1 * sin], axis=-1)


def _rope_2d(x, ang_row, ang_col):
    return jnp.concatenate([_rope_half(x[..., :ROT_AXIS_DIM], ang_row),
                            _rope_half(x[..., ROT_AXIS_DIM:], ang_col)], axis=-1)


def _heads(t, n_heads):
    return t.reshape(t.shape[0], t.shape[1], n_heads, HEAD_DIM)


def _softmax_with_sink(parts, sink):
    b, h, g, q = parts[0].shape[:4]
    sink_logit = jnp.broadcast_to(sink.astype(jnp.float32).reshape(1, h, g, 1, 1), (b, h, g, q, 1))
    probs = jax.nn.softmax(jnp.concatenate([sink_logit] + list(parts), axis=-1), axis=-1)
    splits = [int(s) for s in np.cumsum([1] + [p.shape[-1] for p in parts[:-1]])]
    return jnp.split(probs, splits, axis=-1)[1:]


def _windowed_attention(q, k, v, k_ctx, v_ctx, sink):
    b, n = q.shape[:2]
    nb = n // BLOCK
    span = BLOCK + 2 * WINDOW
    qg = (q * (HEAD_DIM ** -0.5)).reshape(b, n, N_KV_HEADS, GQA_GROUP, HEAD_DIM)
    pad = ((0, 0), (WINDOW, WINDOW), (0, 0), (0, 0))
    k_pad = jnp.pad(k, pad)
    v_pad = jnp.pad(v, pad)
    rel = jnp.arange(span)[None, :] - jnp.arange(BLOCK)[:, None]
    band = (rel >= 0) & (rel <= 2 * WINDOW)

    def block(i):
        start = i * BLOCK
        q_blk = lax.dynamic_slice_in_dim(qg, start, BLOCK, axis=1)
        k_blk = lax.dynamic_slice_in_dim(k_pad, start, span, axis=1)
        v_blk = lax.dynamic_slice_in_dim(v_pad, start, span, axis=1)
        key_pos = start - WINDOW + jnp.arange(span)
        valid = band & ((key_pos >= 0) & (key_pos < n))[None, :]
        s_loc = jnp.einsum('bqhgd,bkhd->bhgqk', q_blk, k_blk).astype(jnp.float32)
        s_loc = jnp.where(valid, s_loc, NEG_INF)
        s_ctx = jnp.einsum('bqhgd,bchd->bhgqc', q_blk, k_ctx).astype(jnp.float32)
        p_loc, p_ctx = _softmax_with_sink([s_loc, s_ctx], sink)
        return (jnp.einsum('bhgqk,bkhd->bqhgd', p_loc.astype(v.dtype), v_blk)
                + jnp.einsum('bhgqc,bchd->bqhgd', p_ctx.astype(v.dtype), v_ctx))

    out = lax.map(block, jnp.arange(nb))
    return jnp.moveaxis(out, 0, 1).reshape(b, n, ATTN_WIDTH)


def _context_attention(q, k_ctx, v_ctx, sink):
    b, m = q.shape[:2]
    qg = (q * (HEAD_DIM ** -0.5)).reshape(b, m, N_KV_HEADS, GQA_GROUP, HEAD_DIM)
    s = jnp.einsum('bqhgd,bkhd->bhgqk', qg, k_ctx).astype(jnp.float32)
    (p,) = _softmax_with_sink([s], sink)
    out = jnp.einsum('bhgqk,bkhd->bqhgd', p.astype(v_ctx.dtype), v_ctx)
    return out.reshape(b, m, ATTN_WIDTH)


def _conformer_conv(a, glu_gate, conv_w, conv_b, ln_g, ln_b):
    u = a * jax.nn.sigmoid(glu_gate)
    u = lax.conv_general_dilated(u, conv_w[:, None, :], window_strides=(1,),
                                 padding=[(CONV_SIZE // 2, CONV_SIZE // 2)],
                                 dimension_numbers=('NWC', 'WIO', 'NWC'),
                                 feature_group_count=CONV_WIDTH) + conv_b
    return jax.nn.silu(_layer_norm(u, ln_g, ln_b))


def _branch_output(p, attn, conv_w, conv_b, ln_g, ln_b, w_out):
    conv = _conformer_conv(p[..., CA_OFF:CB_OFF], p[..., CB_OFF:CG_OFF], conv_w, conv_b, ln_g, ln_b)
    mixed = jnp.concatenate([attn * jax.nn.silu(p[..., AG_OFF:CA_OFF]),
                             conv * jax.nn.silu(p[..., CG_OFF:IN_WIDTH])], axis=-1)
    return mixed @ w_out


def setup_inputs(seed: int = 0) -> dict:
    key = jax.random.key(seed)
    ks = jax.random.split(key, 16)
    f32 = jnp.float32
    d = D_MODEL
    return {
        "x": jax.random.normal(ks[0], (BATCH, SEQ, d), f32),
        "c": jax.random.normal(ks[1], (BATCH, d), f32),
        "ctx": jax.random.normal(ks[2], (BATCH, CTX_LEN, d), f32),
        "c_ctx": jax.random.normal(ks[3], (d,), f32),
        "w_ada": 0.5 * jax.random.normal(ks[4], (DEPTH, d, 3 * d), f32) * d ** -0.5,
        "b_ada": 0.01 * jax.random.normal(ks[5], (DEPTH, 3 * d), f32),
        "w_in": jax.random.normal(ks[6], (DEPTH, d, IN_WIDTH), f32) * d ** -0.5,
        "attn_sink": 0.5 * jax.random.normal(ks[7], (DEPTH, N_Q_HEADS), f32),
        "conv_w": jax.random.normal(ks[8], (DEPTH, CONV_SIZE, CONV_WIDTH), f32) * CONV_SIZE ** -0.5,
        "conv_b": 0.01 * jax.random.normal(ks[9], (DEPTH, CONV_WIDTH), f32),
        "conv_ln_g": 1.0 + 0.01 * jax.random.normal(ks[10], (DEPTH, CONV_WIDTH), f32),
        "conv_ln_b": 0.01 * jax.random.normal(ks[11], (DEPTH, CONV_WIDTH), f32),
        "w_out": BETA * jax.random.normal(ks[12], (DEPTH, d, d), f32) * d ** -0.5,
        "post_ln_g": 1.0 + 0.01 * jax.random.normal(ks[13], (DEPTH, d), f32),
        "post_ln_b": 0.01 * jax.random.normal(ks[14], (DEPTH, d), f32),
    }


def reference(x, c, ctx, c_ctx, w_ada, b_ada, w_in, attn_sink, conv_w, conv_b,
              conv_ln_g, conv_ln_b, w_out, post_ln_g, post_ln_b):
    ang_row, ang_col = _axial_angles(x.shape[1])
    for l in range(DEPTH):
        shift, scale, gate = _adaln(c, w_ada[l], b_ada[l])
        shift_c, scale_c, gate_c = _adaln(c_ctx, w_ada[l], b_ada[l])
        h = _norm(x) * (1.0 + scale[:, None, :]) + shift[:, None, :]
        h_ctx = _norm(ctx) * (1.0 + scale_c) + shift_c
        kv_ctx = h_ctx @ w_in[l][:, K_OFF:AG_OFF]
        k_ctx = _heads(kv_ctx[..., :KV_WIDTH], N_KV_HEADS)
        v_ctx = _heads(kv_ctx[..., KV_WIDTH:], N_KV_HEADS)
        p = h @ w_in[l]
        q = _rope_2d(_heads(p[..., Q_OFF:K_OFF], N_Q_HEADS), ang_row, ang_col)
        k = _rope_2d(_heads(p[..., K_OFF:V_OFF], N_KV_HEADS), ang_row, ang_col)
        v = _heads(p[..., V_OFF:AG_OFF], N_KV_HEADS)
        attn = _windowed_attention(q, k, v, k_ctx, v_ctx, attn_sink[l])
        y = _branch_output(p, attn, conv_w[l], conv_b[l], conv_ln_g[l], conv_ln_b[l], w_out[l])
        x_next = _layer_norm(ALPHA * x + gate[:, None, :] * y, post_ln_g[l], post_ln_b[l])
        if l < DEPTH - 1:
            p_c = h_ctx @ w_in[l]
            attn_c = _context_attention(_heads(p_c[..., Q_OFF:K_OFF], N_Q_HEADS), k_ctx, v_ctx, attn_sink[l])
            y_c = _branch_output(p_c, attn_c, conv_w[l], conv_b[l], conv_ln_g[l], conv_ln_b[l], w_out[l])
            ctx = _layer_norm(ALPHA * ctx + gate_c * y_c, post_ln_g[l], post_ln_b[l])
        x = x_next
    return x
```

```python
import functools

import jax
import jax.numpy as jnp
from jax import lax
from jax.experimental import pallas as pl
from jax.experimental.pallas import tpu as pltpu

F32 = jnp.float32
BF16 = jnp.bfloat16

D_MODEL = 1024
GRID_W = 64
HEAD_DIM = 64
N_Q_HEADS = 8
N_KV_HEADS = 2
GQA_GROUP = N_Q_HEADS // N_KV_HEADS
ATTN_WIDTH = N_Q_HEADS * HEAD_DIM
KV_WIDTH = N_KV_HEADS * HEAD_DIM
CONV_WIDTH = D_MODEL - ATTN_WIDTH
CONV_SIZE = 31
CONV_HALF = CONV_SIZE // 2
WINDOW = 128
QBLK = 128
ROPE_BASE = 10000.0
ROT_AXIS_DIM = HEAD_DIM // 2
ROT_HALF = ROT_AXIS_DIM // 2
LN_EPS = 1e-6
NEG_INF = -1e30
DEPTH = 1
ALPHA = (2.0 * DEPTH) ** 0.25

Q_OFF = 0
K_OFF = Q_OFF + ATTN_WIDTH
V_OFF = K_OFF + KV_WIDTH
AG_OFF = V_OFF + KV_WIDTH
CA_OFF = AG_OFF + ATTN_WIDTH
CB_OFF = CA_OFF + CONV_WIDTH
CG_OFF = CB_OFF + CONV_WIDTH
IN_WIDTH = CG_OFF + CONV_WIDTH

ACT_Q = 0
ACT_U = ATTN_WIDTH
ACT_SAG = ACT_U + CONV_WIDTH
ACT_SCG = ACT_SAG + ATTN_WIDTH
ACT_WIDTH = ACT_SCG + CONV_WIDTH

LANES = 128
U_HALO = 16
TILE_ROWS = 512
ADA_ROWS = 40
ADA_COLS = 512
VMEM_LIMIT = 56 * 1024 * 1024


def _norm_rows(x):
    mu = jnp.mean(x, axis=-1, keepdims=True)
    xc = x - mu
    var = jnp.mean(xc * xc, axis=-1, keepdims=True)
    return xc * lax.rsqrt(var + LN_EPS)


def _silu(x):
    return x * jax.nn.sigmoid(x)


def _adaln_body(c_ref, w_ref, b_ref, o_ref):
    o_ref[...] = jnp.dot(_silu(c_ref[...]), w_ref[...], preferred_element_type=F32) + b_ref[...]


def _adaln(cond, w_ada, b_ada):
    n_cols = w_ada.shape[1]
    return pl.pallas_call(
        _adaln_body,
        grid=(n_cols // ADA_COLS,),
        in_specs=[
            pl.BlockSpec((ADA_ROWS, D_MODEL), lambda j: (0, 0)),
            pl.BlockSpec((D_MODEL, ADA_COLS), lambda j: (0, j)),
            pl.BlockSpec((1, ADA_COLS), lambda j: (0, j)),
        ],
        out_specs=pl.BlockSpec((ADA_ROWS, ADA_COLS), lambda j: (0, j)),
        out_shape=jax.ShapeDtypeStruct((ADA_ROWS, n_cols), F32),
        compiler_params=pltpu.CompilerParams(dimension_semantics=("arbitrary",)),
        name="adaln",
    )(cond, w_ada, b_ada)


def _dup_heads(v):
    lane = lax.broadcasted_iota(jnp.int32, (1, LANES), 1)
    swapped = pltpu.roll(v, HEAD_DIM, 1)
    low = lane < HEAD_DIM
    return jnp.where(low, v, swapped), jnp.where(low, swapped, v)


def _ctx_kv_body(ctx_ref, shift_ref, scale_ref, w_ref, kT_ref, v_ref):
    h = _norm_rows(ctx_ref[...]) * (1.0 + scale_ref[...]) + shift_ref[...]
    kv = jnp.dot(h.astype(BF16), w_ref[...], preferred_element_type=F32)
    kT_ref[...] = kv[:, :KV_WIDTH].T.astype(BF16)
    v0, v1 = _dup_heads(kv[:, KV_WIDTH:])
    v_ref[:, :LANES] = v0.astype(BF16)
    v_ref[:, LANES:] = v1.astype(BF16)


def _ctx_kv(ctx, shift_c, scale_c, w_kv):
    b, m, d = ctx.shape
    return pl.pallas_call(
        _ctx_kv_body,
        grid=(b,),
        in_specs=[
            pl.BlockSpec((None, m, d), lambda i: (i, 0, 0)),
            pl.BlockSpec((1, d), lambda i: (0, 0)),
            pl.BlockSpec((1, d), lambda i: (0, 0)),
            pl.BlockSpec((d, 2 * KV_WIDTH), lambda i: (0, 0)),
        ],
        out_specs=[
            pl.BlockSpec((None, KV_WIDTH, m), lambda i: (i, 0, 0)),
            pl.BlockSpec((None, m, 2 * LANES), lambda i: (i, 0, 0)),
        ],
        out_shape=[
            jax.ShapeDtypeStruct((b, KV_WIDTH, m), BF16),
            jax.ShapeDtypeStruct((b, m, 2 * LANES), BF16),
        ],
        compiler_params=pltpu.CompilerParams(dimension_semantics=("arbitrary",)),
        name="ctx_kv",
    )(ctx, shift_c, scale_c, w_kv)


def _in_proj_body(x_ref, shift_ref, scale_ref, cos_ref, sin_ref, w_ref, act_ref, kT_ref, v_ref):
    h = _norm_rows(x_ref[...]) * (1.0 + scale_ref[...]) + shift_ref[...]
    hb = h.astype(BF16)
    cos = cos_ref[...]
    sin = sin_ref[...]
    lane = lax.broadcasted_iota(jnp.int32, (1, LANES), 1)
    first_half = (lane % ROT_AXIS_DIM) < ROT_HALF

    def rope(t):
        partner = jnp.where(first_half, pltpu.roll(t, LANES - ROT_HALF, 1), pltpu.roll(t, ROT_HALF, 1))
        return t * cos + partner * sin

    def proj(lo, width):
        return jnp.dot(hb, w_ref[:, lo:lo + width], preferred_element_type=F32)

    q = proj(Q_OFF, ATTN_WIDTH)
    for c in range(ATTN_WIDTH // LANES):
        sl = slice(c * LANES, (c + 1) * LANES)
        act_ref[:, ACT_Q + c * LANES:ACT_Q + (c + 1) * LANES] = (rope(q[:, sl]) * (HEAD_DIM ** -0.5)).astype(BF16)

    kv = proj(K_OFF, 2 * KV_WIDTH)
    kT_ref[...] = rope(kv[:, :KV_WIDTH]).T.astype(BF16)
    v0, v1 = _dup_heads(kv[:, KV_WIDTH:])
    v_ref[:, :LANES] = v0.astype(BF16)
    v_ref[:, LANES:] = v1.astype(BF16)

    act_ref[:, ACT_SAG:ACT_SAG + ATTN_WIDTH] = _silu(proj(AG_OFF, ATTN_WIDTH)).astype(BF16)
    glu = proj(CA_OFF, 2 * CONV_WIDTH)
    act_ref[:, ACT_U:ACT_U + CONV_WIDTH] = (glu[:, :CONV_WIDTH] * jax.nn.sigmoid(glu[:, CONV_WIDTH:])).astype(BF16)
    act_ref[:, ACT_SCG:ACT_SCG + CONV_WIDTH] = _silu(proj(CG_OFF, CONV_WIDTH)).astype(BF16)


def _in_proj(x, shift, scale, cos_t, sin_t, w_in):
    b, n, d = x.shape
    tm = TILE_ROWS
    return pl.pallas_call(
        _in_proj_body,
        grid=(b, n // tm),
        in_specs=[
            pl.BlockSpec((None, tm, d), lambda bi, i: (bi, i, 0)),
            pl.BlockSpec((None, 1, d), lambda bi, i: (bi, 0, 0)),
            pl.BlockSpec((None, 1, d), lambda bi, i: (bi, 0, 0)),
            pl.BlockSpec((tm, LANES), lambda bi, i: (i, 0)),
            pl.BlockSpec((tm, LANES), lambda bi, i: (i, 0)),
            pl.BlockSpec((d, IN_WIDTH), lambda bi, i: (0, 0)),
        ],
        out_specs=[
            pl.BlockSpec((None, tm, ACT_WIDTH), lambda bi, i: (bi, i, 0)),
            pl.BlockSpec((None, KV_WIDTH, tm), lambda bi, i: (bi, 0, i)),
            pl.BlockSpec((None, tm, 2 * LANES), lambda bi, i: (bi, i, 0)),
        ],
        out_shape=[
            jax.ShapeDtypeStruct((b, n, ACT_WIDTH), BF16),
            jax.ShapeDtypeStruct((b, KV_WIDTH, n), BF16),
            jax.ShapeDtypeStruct((b, n, 2 * LANES), BF16),
        ],
        compiler_params=pltpu.CompilerParams(
            dimension_semantics=("arbitrary", "arbitrary"), vmem_limit_bytes=VMEM_LIMIT),
        name="in_proj",
    )(x, shift, scale, cos_t, sin_t, w_in)


def _mixer_body(sink_ref, act_ref, u_prev_ref, u_next_ref,
                kT_ref, kT_prev_ref, kT_next_ref, v_ref, v_prev_ref, v_next_ref,
                kcT_ref, vc_ref, band_ref,
                conv_w_ref, conv_b_ref, cln_g_ref, cln_b_ref, w_out_ref,
                x_ref, gate_ref, pln_g_ref, pln_b_ref,
                o_ref,
                kT4_ref, v4_ref, kcT4_ref, vc4_ref, mixed_ref, ubuf_ref):
    tq = x_ref.shape[0]
    i = pl.program_id(1)
    is_first = i == 0
    is_last = i == pl.num_programs(1) - 1
    n_qblk = tq // QBLK

    for h in range(N_KV_HEADS):
        rows = slice(h * HEAD_DIM, (h + 1) * HEAD_DIM)
        for g in range(GQA_GROUP):
            dst = slice(g * HEAD_DIM, (g + 1) * HEAD_DIM)
            kT4_ref[h, dst, 0:WINDOW] = kT_prev_ref[rows, :]
            kT4_ref[h, dst, WINDOW:WINDOW + tq] = kT_ref[rows, :]
            kT4_ref[h, dst, WINDOW + tq:] = kT_next_ref[rows, :]
            kcT4_ref[h, dst, :] = kcT_ref[rows, :]
        lanes = slice(h * LANES, (h + 1) * LANES)
        for half in range(2):
            dst = slice(half * LANES, (half + 1) * LANES)
            v4_ref[h, 0:WINDOW, dst] = v_prev_ref[:, lanes]
            v4_ref[h, WINDOW:WINDOW + tq, dst] = v_ref[:, lanes]
            v4_ref[h, WINDOW + tq:, dst] = v_next_ref[:, lanes]
            vc4_ref[h, :, dst] = vc_ref[:, lanes]

    lane_head = lax.broadcasted_iota(jnp.int32, (1, GQA_GROUP * HEAD_DIM), 1) // HEAD_DIM
    col = lax.broadcasted_iota(jnp.int32, (1, QBLK + 2 * WINDOW), 1)

    for j in range(n_qblk):
        bias = band_ref[...]
        if j == 0:
            bias = bias + jnp.where(col < WINDOW, jnp.where(is_first, NEG_INF, 0.0), 0.0)
        if j == n_qblk - 1:
            bias = bias + jnp.where(col >= QBLK + WINDOW, jnp.where(is_last, NEG_INF, 0.0), 0.0)
        qrows = slice(j * QBLK, (j + 1) * QBLK)
        kcols = slice(j * QBLK, j * QBLK + QBLK + 2 * WINDOW)
        for h in range(N_KV_HEADS):
            hl = slice(h * GQA_GROUP * HEAD_DIM, (h + 1) * GQA_GROUP * HEAD_DIM)
            qb = act_ref[qrows, ACT_Q + hl.start:ACT_Q + hl.stop]
            lhs = jnp.concatenate(
                [jnp.where(lane_head == g, qb, jnp.zeros_like(qb)) for g in range(GQA_GROUP)], axis=0)
            s_loc = jnp.dot(lhs, kT4_ref[h, :, kcols], preferred_element_type=F32)
            s_ctx = jnp.dot(lhs, kcT4_ref[h], preferred_element_type=F32)
            s_loc = (s_loc.reshape(GQA_GROUP, QBLK, -1) + bias[None]).reshape(GQA_GROUP * QBLK, -1)
            sink = jnp.concatenate(
                [jnp.full((QBLK, 1), sink_ref[h * GQA_GROUP + g], F32) for g in range(GQA_GROUP)], axis=0)
            m = jnp.maximum(jnp.maximum(jnp.max(s_loc, axis=-1, keepdims=True),
                                        jnp.max(s_ctx, axis=-1, keepdims=True)), sink)
            p_loc = jnp.exp(s_loc - m)
            p_ctx = jnp.exp(s_ctx - m)
            denom = (jnp.sum(p_loc, axis=-1, keepdims=True) + jnp.sum(p_ctx, axis=-1, keepdims=True)
                     + jnp.exp(sink - m))
            r = (jnp.dot(p_loc.astype(BF16), v4_ref[h, kcols, :], preferred_element_type=F32)
                 + jnp.dot(p_ctx.astype(BF16), vc4_ref[h], preferred_element_type=F32))
            r = r / denom
            attn = jnp.zeros((QBLK, GQA_GROUP * HEAD_DIM), F32)
            for g in range(GQA_GROUP):
                attn = jnp.where(lane_head == g, r[g * QBLK:(g + 1) * QBLK], attn)
            sag = act_ref[qrows, ACT_SAG + hl.start:ACT_SAG + hl.stop].astype(F32)
            mixed_ref[qrows, hl] = (attn * sag).astype(BF16)

    u_prev = u_prev_ref[...].astype(F32)
    u_next = u_next_ref[...].astype(F32)
    ubuf_ref[0:U_HALO, :] = jnp.where(is_first, jnp.zeros_like(u_prev), u_prev)
    ubuf_ref[U_HALO:U_HALO + tq, :] = act_ref[:, ACT_U:ACT_U + CONV_WIDTH].astype(F32)
    ubuf_ref[U_HALO + tq:, :] = jnp.where(is_last, jnp.zeros_like(u_next), u_next)
    conv_rows = 32
    for r0 in range(0, tq, conv_rows):
        acc = jnp.zeros((conv_rows, CONV_WIDTH), F32) + conv_b_ref[...]
        for t in range(CONV_SIZE):
            start = r0 + U_HALO - CONV_HALF + t
            acc = acc + ubuf_ref[start:start + conv_rows, :] * conv_w_ref[t:t + 1, :]
        cn = _silu(_norm_rows(acc) * cln_g_ref[...] + cln_b_ref[...])
        scg = act_ref[r0:r0 + conv_rows, ACT_SCG:ACT_SCG + CONV_WIDTH].astype(F32)
        mixed_ref[r0:r0 + conv_rows, ATTN_WIDTH:] = (cn * scg).astype(BF16)

    y = jnp.dot(mixed_ref[...], w_out_ref[...], preferred_element_type=F32)
    z = ALPHA * x_ref[...] + gate_ref[...] * y
    o_ref[...] = _norm_rows(z) * pln_g_ref[...] + pln_b_ref[...]


def _mixer(sink, act, kT, vv, kcT, vc, band, conv_w, conv_b, cln_g, cln_b, w_out, x, gate, pln_g, pln_b):
    b, n, d = x.shape
    tq = TILE_ROWS
    m = kcT.shape[-1]
    qpt = tq // QBLK
    upt = tq // U_HALO
    n_q = n // QBLK
    n_u = n // U_HALO
    u_col = ACT_U // CONV_WIDTH
    const2 = lambda bi, i: (0, 0)
    return pl.pallas_call(
        _mixer_body,
        grid=(b, n // tq),
        in_specs=[
            pl.BlockSpec(memory_space=pltpu.SMEM),
            pl.BlockSpec((None, tq, ACT_WIDTH), lambda bi, i: (bi, i, 0)),
            pl.BlockSpec((None, U_HALO, CONV_WIDTH), lambda bi, i: (bi, jnp.maximum(i * upt - 1, 0), u_col)),
            pl.BlockSpec((None, U_HALO, CONV_WIDTH), lambda bi, i: (bi, jnp.minimum((i + 1) * upt, n_u - 1), u_col)),
            pl.BlockSpec((None, KV_WIDTH, tq), lambda bi, i: (bi, 0, i)),
            pl.BlockSpec((None, KV_WIDTH, WINDOW), lambda bi, i: (bi, 0, jnp.maximum(i * qpt - 1, 0))),
            pl.BlockSpec((None, KV_WIDTH, WINDOW), lambda bi, i: (bi, 0, jnp.minimum((i + 1) * qpt, n_q - 1))),
            pl.BlockSpec((None, tq, 2 * LANES), lambda bi, i: (bi, i, 0)),
            pl.BlockSpec((None, WINDOW, 2 * LANES), lambda bi, i: (bi, jnp.maximum(i * qpt - 1, 0), 0)),
            pl.BlockSpec((None, WINDOW, 2 * LANES), lambda bi, i: (bi, jnp.minimum((i + 1) * qpt, n_q - 1), 0)),
            pl.BlockSpec((None, KV_WIDTH, m), lambda bi, i: (bi, 0, 0)),
            pl.BlockSpec((None, m, 2 * LANES), lambda bi, i: (bi, 0, 0)),
            pl.BlockSpec((QBLK, QBLK + 2 * WINDOW), const2),
            pl.BlockSpec((CONV_SIZE, CONV_WIDTH), const2),
            pl.BlockSpec((1, CONV_WIDTH), const2),
            pl.BlockSpec((1, CONV_WIDTH), const2),
            pl.BlockSpec((1, CONV_WIDTH), const2),
            pl.BlockSpec((d, d), const2),
            pl.BlockSpec((None, tq, d), lambda bi, i: (bi, i, 0)),
            pl.BlockSpec((None, 1, d), lambda bi, i: (bi, 0, 0)),
            pl.BlockSpec((1, d), const2),
            pl.BlockSpec((1, d), const2),
        ],
        out_specs=pl.BlockSpec((None, tq, d), lambda bi, i: (bi, i, 0)),
        out_shape=jax.ShapeDtypeStruct((b, n, d), F32),
        scratch_shapes=[
            pltpu.VMEM((N_KV_HEADS, GQA_GROUP * HEAD_DIM, tq + 2 * WINDOW), BF16),
            pltpu.VMEM((N_KV_HEADS, tq + 2 * WINDOW, 2 * LANES), BF16),
            pltpu.VMEM((N_KV_HEADS, GQA_GROUP * HEAD_DIM, m), BF16),
            pltpu.VMEM((N_KV_HEADS, m, 2 * LANES), BF16),
            pltpu.VMEM((tq, d), BF16),
            pltpu.VMEM((tq + 2 * U_HALO, CONV_WIDTH), F32),
        ],
        compiler_params=pltpu.CompilerParams(
            dimension_semantics=("arbitrary", "arbitrary"), vmem_limit_bytes=VMEM_LIMIT),
        name="mixer",
    )(sink, act, act, act, kT, kT, kT, vv, vv, vv, kcT, vc, band,
      conv_w, conv_b, cln_g, cln_b, w_out, x, gate, pln_g, pln_b)


def _rope_tables(n):
    rows = n // GRID_W
    row = jnp.repeat(jnp.arange(rows, dtype=F32), GRID_W)
    colp = jnp.tile(jnp.arange(GRID_W, dtype=F32), rows)
    inv_freq = ROPE_BASE ** (-jnp.arange(0, ROT_AXIS_DIM, 2, dtype=F32) / ROT_AXIS_DIM)
    ang_row = row[:, None] * inv_freq
    ang_col = colp[:, None] * inv_freq
    cos_h = jnp.concatenate([jnp.cos(ang_row), jnp.cos(ang_row), jnp.cos(ang_col), jnp.cos(ang_col)], axis=-1)
    sin_h = jnp.concatenate([-jnp.sin(ang_row), jnp.sin(ang_row), -jnp.sin(ang_col), jnp.sin(ang_col)], axis=-1)
    reps = LANES // HEAD_DIM
    return jnp.tile(cos_h, (1, reps)), jnp.tile(sin_h, (1, reps))


def _band_bias():
    rel = jnp.arange(QBLK + 2 * WINDOW)[None, :] - jnp.arange(QBLK)[:, None]
    return jnp.where((rel >= 0) & (rel <= 2 * WINDOW), 0.0, NEG_INF).astype(F32)


def kernel(x, c, ctx, c_ctx, w_ada, b_ada, w_in, attn_sink, conv_w, conv_b, conv_ln_g, conv_ln_b,
           w_out, post_ln_g, post_ln_b):
    assert w_ada.shape[0] == DEPTH
    b, n, d = x.shape
    cond = jnp.zeros((ADA_ROWS, d), F32).at[:b].set(c).at[b].set(c_ctx)
    mods = _adaln(cond, w_ada[0], b_ada[0][None, :])
    shift, scale, gate = (mods[:b, k * d:(k + 1) * d][:, None, :] for k in range(3))
    shift_c, scale_c = (mods[b:b + 1, k * d:(k + 1) * d] for k in range(2))

    w_in_b = w_in[0].astype(BF16)
    kcT, vc = _ctx_kv(ctx, shift_c, scale_c, w_in_b[:, K_OFF:AG_OFF])
    cos_t, sin_t = _rope_tables(n)
    act, kT, vv = _in_proj(x, shift, scale, cos_t, sin_t, w_in_b)
    return _mixer(attn_sink[0], act, kT, vv, kcT, vc, _band_bias(),
                  conv_w[0], conv_b[0][None, :], conv_ln_g[0][None, :], conv_ln_b[0][None, :],
                  w_out[0].astype(BF16), x, gate, post_ln_g[0][None, :], post_ln_b[0][None, :])
```

```python
import functools
import math

import jax
import jax.numpy as jnp
from jax import lax
from jax.experimental import pallas as pl
from jax.experimental.pallas import tpu as pltpu

F32 = jnp.float32
BF16 = jnp.bfloat16

D_MODEL = 1024
GRID_W = 64
HEAD_DIM = 64
N_Q_HEADS = 8
N_KV_HEADS = 2
GQA_GROUP = N_Q_HEADS // N_KV_HEADS
ATTN_WIDTH = N_Q_HEADS * HEAD_DIM
KV_WIDTH = N_KV_HEADS * HEAD_DIM
CONV_WIDTH = D_MODEL - ATTN_WIDTH
CONV_SIZE = 31
CONV_HALF = CONV_SIZE // 2
WINDOW = 128
QBLK = 128
ROPE_BASE = 10000.0
ROT_AXIS_DIM = HEAD_DIM // 2
ROT_HALF = ROT_AXIS_DIM // 2
LN_EPS = 1e-6
NEG_INF = -1e30
DEPTH = 1
ALPHA = (2.0 * DEPTH) ** 0.25

Q_OFF = 0
K_OFF = Q_OFF + ATTN_WIDTH
V_OFF = K_OFF + KV_WIDTH
AG_OFF = V_OFF + KV_WIDTH
CA_OFF = AG_OFF + ATTN_WIDTH
CB_OFF = CA_OFF + CONV_WIDTH
CG_OFF = CB_OFF + CONV_WIDTH
IN_WIDTH = CG_OFF + CONV_WIDTH

ACT_Q = 0
ACT_U = ATTN_WIDTH
ACT_SAG = ACT_U + CONV_WIDTH
ACT_SCG = ACT_SAG + ATTN_WIDTH
ACT_WIDTH = ACT_SCG + CONV_WIDTH

LANES = 128
SUBLANES = 8
CONV_ROWS = 64
CONV_COLS = 256
LOG2E = math.log2(math.e)
U_HALO = 16
CONV_WIN_EXTRA = ((U_HALO + CONV_HALF) // SUBLANES) * SUBLANES
TILE_ROWS = 512
ADA_ROWS = 40
ADA_COLS = 512
VMEM_LIMIT = 56 * 1024 * 1024


def _norm_rows(x):
    mu = jnp.mean(x, axis=-1, keepdims=True)
    xc = x - mu
    var = jnp.mean(xc * xc, axis=-1, keepdims=True)
    return xc * lax.rsqrt(var + LN_EPS)


def _silu(x):
    return x * jax.nn.sigmoid(x)


def _adaln_body(c_ref, w_ref, b_ref, o_ref):
    o_ref[...] = jnp.dot(_silu(c_ref[...]), w_ref[...], preferred_element_type=F32) + b_ref[...]


def _adaln(cond, w_ada, b_ada):
    n_cols = w_ada.shape[1]
    return pl.pallas_call(
        _adaln_body,
        grid=(n_cols // ADA_COLS,),
        in_specs=[
            pl.BlockSpec((ADA_ROWS, D_MODEL), lambda j: (0, 0)),
            pl.BlockSpec((D_MODEL, ADA_COLS), lambda j: (0, j)),
            pl.BlockSpec((1, ADA_COLS), lambda j: (0, j)),
        ],
        out_specs=pl.BlockSpec((ADA_ROWS, ADA_COLS), lambda j: (0, j)),
        out_shape=jax.ShapeDtypeStruct((ADA_ROWS, n_cols), F32),
        compiler_params=pltpu.CompilerParams(dimension_semantics=("arbitrary",)),
        name="adaln",
    )(cond, w_ada, b_ada)


def _dup_heads(v):
    lane = lax.broadcasted_iota(jnp.int32, (1, LANES), 1)
    swapped = pltpu.roll(v, HEAD_DIM, 1)
    low = lane < HEAD_DIM
    return jnp.where(low, v, swapped), jnp.where(low, swapped, v)


def _ctx_kv_body(ctx_ref, shift_ref, scale_ref, w_ref, kT_ref, v_ref):
    h = _norm_rows(ctx_ref[...]) * (1.0 + scale_ref[...]) + shift_ref[...]
    kv = jnp.dot(h.astype(BF16), w_ref[...], preferred_element_type=F32)
    kT_ref[...] = kv[:, :KV_WIDTH].T.astype(BF16)
    v0, v1 = _dup_heads(kv[:, KV_WIDTH:])
    v_ref[:, :LANES] = v0.astype(BF16)
    v_ref[:, LANES:] = v1.astype(BF16)


def _ctx_kv(ctx, shift_c, scale_c, w_kv):
    b, m, d = ctx.shape
    return pl.pallas_call(
        _ctx_kv_body,
        grid=(b,),
        in_specs=[
            pl.BlockSpec((None, m, d), lambda i: (i, 0, 0)),
            pl.BlockSpec((1, d), lambda i: (0, 0)),
            pl.BlockSpec((1, d), lambda i: (0, 0)),
            pl.BlockSpec((d, 2 * KV_WIDTH), lambda i: (0, 0)),
        ],
        out_specs=[
            pl.BlockSpec((None, KV_WIDTH, m), lambda i: (i, 0, 0)),
            pl.BlockSpec((None, m, 2 * LANES), lambda i: (i, 0, 0)),
        ],
        out_shape=[
            jax.ShapeDtypeStruct((b, KV_WIDTH, m), BF16),
            jax.ShapeDtypeStruct((b, m, 2 * LANES), BF16),
        ],
        compiler_params=pltpu.CompilerParams(dimension_semantics=("arbitrary",)),
        name="ctx_kv",
    )(ctx, shift_c, scale_c, w_kv)


def _in_proj_body(x_ref, shift_ref, scale_ref, cos_ref, sin_ref, w_ref, act_ref, kT_ref, v_ref):
    h = _norm_rows(x_ref[...]) * (1.0 + scale_ref[...]) + shift_ref[...]
    hb = h.astype(BF16)
    cos = cos_ref[...]
    sin = sin_ref[...]
    lane = lax.broadcasted_iota(jnp.int32, (1, LANES), 1)
    first_half = (lane % ROT_AXIS_DIM) < ROT_HALF

    def rope(t):
        partner = jnp.where(first_half, pltpu.roll(t, LANES - ROT_HALF, 1), pltpu.roll(t, ROT_HALF, 1))
        return t * cos + partner * sin

    def proj(lo, width):
        return jnp.dot(hb, w_ref[:, lo:lo + width], preferred_element_type=F32)

    q = proj(Q_OFF, ATTN_WIDTH)
    for c in range(ATTN_WIDTH // LANES):
        sl = slice(c * LANES, (c + 1) * LANES)
        act_ref[:, ACT_Q + c * LANES:ACT_Q + (c + 1) * LANES] = (rope(q[:, sl]) * (HEAD_DIM ** -0.5 * LOG2E)).astype(BF16)

    kv = proj(K_OFF, 2 * KV_WIDTH)
    kT_ref[...] = rope(kv[:, :KV_WIDTH]).T.astype(BF16)
    v0, v1 = _dup_heads(kv[:, KV_WIDTH:])
    v_ref[:, :LANES] = v0.astype(BF16)
    v_ref[:, LANES:] = v1.astype(BF16)

    act_ref[:, ACT_SAG:ACT_SAG + ATTN_WIDTH] = _silu(proj(AG_OFF, ATTN_WIDTH)).astype(BF16)
    glu = proj(CA_OFF, 2 * CONV_WIDTH)
    act_ref[:, ACT_U:ACT_U + CONV_WIDTH] = (glu[:, :CONV_WIDTH] * jax.nn.sigmoid(glu[:, CONV_WIDTH:])).astype(BF16)
    act_ref[:, ACT_SCG:ACT_SCG + CONV_WIDTH] = _silu(proj(CG_OFF, CONV_WIDTH)).astype(BF16)


def _in_proj(x, shift, scale, cos_t, sin_t, w_in):
    b, n, d = x.shape
    tm = TILE_ROWS
    return pl.pallas_call(
        _in_proj_body,
        grid=(b, n // tm),
        in_specs=[
            pl.BlockSpec((None, tm, d), lambda bi, i: (bi, i, 0)),
            pl.BlockSpec((None, 1, d), lambda bi, i: (bi, 0, 0)),
            pl.BlockSpec((None, 1, d), lambda bi, i: (bi, 0, 0)),
            pl.BlockSpec((tm, LANES), lambda bi, i: (i, 0)),
            pl.BlockSpec((tm, LANES), lambda bi, i: (i, 0)),
            pl.BlockSpec((d, IN_WIDTH), lambda bi, i: (0, 0)),
        ],
        out_specs=[
            pl.BlockSpec((None, tm, ACT_WIDTH), lambda bi, i: (bi, i, 0)),
            pl.BlockSpec((None, KV_WIDTH, tm), lambda bi, i: (bi, 0, i)),
            pl.BlockSpec((None, tm, 2 * LANES), lambda bi, i: (bi, i, 0)),
        ],
        out_shape=[
            jax.ShapeDtypeStruct((b, n, ACT_WIDTH), BF16),
            jax.ShapeDtypeStruct((b, KV_WIDTH, n), BF16),
            jax.ShapeDtypeStruct((b, n, 2 * LANES), BF16),
        ],
        compiler_params=pltpu.CompilerParams(
            dimension_semantics=("arbitrary", "arbitrary"), vmem_limit_bytes=VMEM_LIMIT),
        name="in_proj",
    )(x, shift, scale, cos_t, sin_t, w_in)


def _mixer_body(sink_ref, act_ref, u_prev_ref, u_next_ref,
                kT_ref, kT_prev_ref, kT_next_ref, v_ref, v_prev_ref, v_next_ref,
                kcT_ref, vc_ref, band_ref,
                conv_w_ref, conv_b_ref, cln_g_ref, cln_b_ref, w_out_ref,
                x_ref, gate_ref, pln_g_ref, pln_b_ref,
                o_ref,
                kT4_ref, v4_ref, kcT4_ref, vc4_ref, mixed_ref, ubuf_ref, ushift_ref, cbuf_ref):
    tq = x_ref.shape[0]
    i = pl.program_id(1)
    is_first = i == 0
    is_last = i == pl.num_programs(1) - 1
    n_qblk = tq // QBLK

    for h in range(N_KV_HEADS):
        rows = slice(h * HEAD_DIM, (h + 1) * HEAD_DIM)
        for g in range(GQA_GROUP):
            dst = slice(g * HEAD_DIM, (g + 1) * HEAD_DIM)
            kT4_ref[h, dst, 0:WINDOW] = kT_prev_ref[rows, :]
            kT4_ref[h, dst, WINDOW:WINDOW + tq] = kT_ref[rows, :]
            kT4_ref[h, dst, WINDOW + tq:] = kT_next_ref[rows, :]
            kcT4_ref[h, dst, :] = kcT_ref[rows, :]
        lanes = slice(h * LANES, (h + 1) * LANES)
        for half in range(2):
            dst = slice(half * LANES, (half + 1) * LANES)
            v4_ref[h, 0:WINDOW, dst] = v_prev_ref[:, lanes]
            v4_ref[h, WINDOW:WINDOW + tq, dst] = v_ref[:, lanes]
            v4_ref[h, WINDOW + tq:, dst] = v_next_ref[:, lanes]
            vc4_ref[h, :, dst] = vc_ref[:, lanes]

    lane_head = lax.broadcasted_iota(jnp.int32, (1, GQA_GROUP * HEAD_DIM), 1) // HEAD_DIM

    def add_bias(t, bias):
        return (t.reshape(GQA_GROUP, QBLK, -1) + bias[None]).reshape(GQA_GROUP * QBLK, -1)

    for j in range(n_qblk):
        bias_lo = band_ref[:, 0:WINDOW]
        bias_hi = band_ref[:, QBLK + WINDOW:]
        if j == 0:
            bias_lo = bias_lo + jnp.where(is_first, NEG_INF, 0.0)
        if j == n_qblk - 1:
            bias_hi = bias_hi + jnp.where(is_last, NEG_INF, 0.0)
        qrows = slice(j * QBLK, (j + 1) * QBLK)
        kcols = slice(j * QBLK, j * QBLK + QBLK + 2 * WINDOW)
        for h in range(N_KV_HEADS):
            hl = slice(h * GQA_GROUP * HEAD_DIM, (h + 1) * GQA_GROUP * HEAD_DIM)
            qb = act_ref[qrows, ACT_Q + hl.start:ACT_Q + hl.stop]
            lhs = jnp.concatenate(
                [jnp.where(lane_head == g, qb, jnp.zeros_like(qb)) for g in range(GQA_GROUP)], axis=0)
            s_loc = jnp.dot(lhs, kT4_ref[h, :, kcols], preferred_element_type=F32)
            s_ctx = jnp.dot(lhs, kcT4_ref[h], preferred_element_type=F32)
            parts = [add_bias(s_loc[:, 0:WINDOW], bias_lo), s_loc[:, WINDOW:WINDOW + QBLK],
                     add_bias(s_loc[:, WINDOW + QBLK:], bias_hi)]
            n_loc = len(parts)
            parts += [s_ctx[:, k * LANES:(k + 1) * LANES] for k in range(s_ctx.shape[1] // LANES)]
            sink = jnp.concatenate(
                [jnp.full((QBLK, 1), sink_ref[h * GQA_GROUP + g] * LOG2E, F32) for g in range(GQA_GROUP)],
                axis=0)
            m = functools.reduce(jnp.maximum, parts)
            m = jnp.maximum(jnp.max(m, axis=-1, keepdims=True), sink)
            probs = [jnp.exp2(t - m) for t in parts]
            denom = jnp.sum(functools.reduce(jnp.add, probs), axis=-1, keepdims=True) + jnp.exp2(sink - m)
            p_loc = jnp.concatenate(probs[:n_loc], axis=1).astype(BF16)
            p_ctx = jnp.concatenate(probs[n_loc:], axis=1).astype(BF16)
            r = (jnp.dot(p_loc, v4_ref[h, kcols, :], preferred_element_type=F32)
                 + jnp.dot(p_ctx, vc4_ref[h], preferred_element_type=F32))
            r = r * (1.0 / denom)
            attn = jnp.zeros((QBLK, GQA_GROUP * HEAD_DIM), F32)
            for g in range(GQA_GROUP):
                attn = jnp.where(lane_head == g, r[g * QBLK:(g + 1) * QBLK], attn)
            sag = act_ref[qrows, ACT_SAG + hl.start:ACT_SAG + hl.stop].astype(F32)
            mixed_ref[qrows, hl] = (attn * sag).astype(BF16)

    u_prev = u_prev_ref[...].astype(F32)
    u_next = u_next_ref[...].astype(F32)
    ubuf_ref[0:U_HALO, :] = jnp.where(is_first, jnp.zeros_like(u_prev), u_prev)
    ubuf_ref[U_HALO:U_HALO + tq, :] = act_ref[:, ACT_U:ACT_U + CONV_WIDTH].astype(F32)
    ubuf_ref[U_HALO + tq:, :] = jnp.where(is_last, jnp.zeros_like(u_next), u_next)
    shift_rows = ushift_ref.shape[1]
    for r in range(1, SUBLANES):
        ushift_ref[r - 1] = ubuf_ref[r:r + shift_rows, :]

    def conv_chunk(ci, carry):
        r0 = pl.multiple_of(ci * CONV_ROWS, CONV_ROWS)
        groups = CONV_ROWS // SUBLANES
        for cb in range(CONV_WIDTH // CONV_COLS):
            ch = slice(cb * CONV_COLS, (cb + 1) * CONV_COLS)
            acc = jnp.zeros((groups, SUBLANES, CONV_COLS), F32) + conv_b_ref[:, ch]
            for r in range(SUBLANES):
                taps = [t for t in range(CONV_SIZE) if (U_HALO - CONV_HALF + t) % SUBLANES == r]
                src = ubuf_ref if r == 0 else ushift_ref.at[r - 1]
                win = src[pl.ds(r0, CONV_ROWS + CONV_WIN_EXTRA), ch].reshape(-1, SUBLANES, CONV_COLS)
                for t in taps:
                    a = (U_HALO - CONV_HALF + t) // SUBLANES
                    acc = acc + win[a:a + groups] * conv_w_ref[t, :, ch][None]
            cbuf_ref[pl.ds(r0, CONV_ROWS), ch] = acc.reshape(CONV_ROWS, CONV_COLS)
        return carry

    lax.fori_loop(0, tq // CONV_ROWS, conv_chunk, 0)
    cn = _silu(_norm_rows(cbuf_ref[...]) * cln_g_ref[...] + cln_b_ref[...])
    scg = act_ref[:, ACT_SCG:ACT_SCG + CONV_WIDTH].astype(F32)
    mixed_ref[:, ATTN_WIDTH:] = (cn * scg).astype(BF16)

    y = jnp.dot(mixed_ref[...], w_out_ref[...], preferred_element_type=F32)
    z = ALPHA * x_ref[...] + gate_ref[...] * y
    o_ref[...] = _norm_rows(z) * pln_g_ref[...] + pln_b_ref[...]


def _mixer(sink, act, kT, vv, kcT, vc, band, conv_w, conv_b, cln_g, cln_b, w_out, x, gate, pln_g, pln_b):
    b, n, d = x.shape
    tq = TILE_ROWS
    m = kcT.shape[-1]
    qpt = tq // QBLK
    upt = tq // U_HALO
    n_q = n // QBLK
    n_u = n // U_HALO
    u_col = ACT_U // CONV_WIDTH
    const2 = lambda bi, i: (0, 0)
    return pl.pallas_call(
        _mixer_body,
        grid=(b, n // tq),
        in_specs=[
            pl.BlockSpec(memory_space=pltpu.SMEM),
            pl.BlockSpec((None, tq, ACT_WIDTH), lambda bi, i: (bi, i, 0)),
            pl.BlockSpec((None, U_HALO, CONV_WIDTH), lambda bi, i: (bi, jnp.maximum(i * upt - 1, 0), u_col)),
            pl.BlockSpec((None, U_HALO, CONV_WIDTH), lambda bi, i: (bi, jnp.minimum((i + 1) * upt, n_u - 1), u_col)),
            pl.BlockSpec((None, KV_WIDTH, tq), lambda bi, i: (bi, 0, i)),
            pl.BlockSpec((None, KV_WIDTH, WINDOW), lambda bi, i: (bi, 0, jnp.maximum(i * qpt - 1, 0))),
            pl.BlockSpec((None, KV_WIDTH, WINDOW), lambda bi, i: (bi, 0, jnp.minimum((i + 1) * qpt, n_q - 1))),
            pl.BlockSpec((None, tq, 2 * LANES), lambda bi, i: (bi, i, 0)),
            pl.BlockSpec((None, WINDOW, 2 * LANES), lambda bi, i: (bi, jnp.maximum(i * qpt - 1, 0), 0)),
            pl.BlockSpec((None, WINDOW, 2 * LANES), lambda bi, i: (bi, jnp.minimum((i + 1) * qpt, n_q - 1), 0)),
            pl.BlockSpec((None, KV_WIDTH, m), lambda bi, i: (bi, 0, 0)),
            pl.BlockSpec((None, m, 2 * LANES), lambda bi, i: (bi, 0, 0)),
            pl.BlockSpec((QBLK, QBLK + 2 * WINDOW), const2),
            pl.BlockSpec((CONV_SIZE, SUBLANES, CONV_WIDTH), lambda bi, i: (0, 0, 0)),
            pl.BlockSpec((1, CONV_WIDTH), const2),
            pl.BlockSpec((1, CONV_WIDTH), const2),
            pl.BlockSpec((1, CONV_WIDTH), const2),
            pl.BlockSpec((d, d), const2),
            pl.BlockSpec((None, tq, d), lambda bi, i: (bi, i, 0)),
            pl.BlockSpec((None, 1, d), lambda bi, i: (bi, 0, 0)),
            pl.BlockSpec((1, d), const2),
            pl.BlockSpec((1, d), const2),
        ],
        out_specs=pl.BlockSpec((None, tq, d), lambda bi, i: (bi, i, 0)),
        out_shape=jax.ShapeDtypeStruct((b, n, d), F32),
        scratch_shapes=[
            pltpu.VMEM((N_KV_HEADS, GQA_GROUP * HEAD_DIM, tq + 2 * WINDOW), BF16),
            pltpu.VMEM((N_KV_HEADS, tq + 2 * WINDOW, 2 * LANES), BF16),
            pltpu.VMEM((N_KV_HEADS, GQA_GROUP * HEAD_DIM, m), BF16),
            pltpu.VMEM((N_KV_HEADS, m, 2 * LANES), BF16),
            pltpu.VMEM((tq, d), BF16),
            pltpu.VMEM((tq + 2 * U_HALO, CONV_WIDTH), F32),
            pltpu.VMEM((SUBLANES - 1, tq + 2 * U_HALO - SUBLANES, CONV_WIDTH), F32),
            pltpu.VMEM((tq, CONV_WIDTH), F32),
        ],
        compiler_params=pltpu.CompilerParams(
            dimension_semantics=("arbitrary", "arbitrary"), vmem_limit_bytes=VMEM_LIMIT),
        name="mixer",
    )(sink, act, act, act, kT, kT, kT, vv, vv, vv, kcT, vc, band,
      conv_w, conv_b, cln_g, cln_b, w_out, x, gate, pln_g, pln_b)


def _rope_tables(n):
    rows = n // GRID_W
    row = jnp.repeat(jnp.arange(rows, dtype=F32), GRID_W)
    colp = jnp.tile(jnp.arange(GRID_W, dtype=F32), rows)
    inv_freq = ROPE_BASE ** (-jnp.arange(0, ROT_AXIS_DIM, 2, dtype=F32) / ROT_AXIS_DIM)
    ang_row = row[:, None] * inv_freq
    ang_col = colp[:, None] * inv_freq
    cos_h = jnp.concatenate([jnp.cos(ang_row), jnp.cos(ang_row), jnp.cos(ang_col), jnp.cos(ang_col)], axis=-1)
    sin_h = jnp.concatenate([-jnp.sin(ang_row), jnp.sin(ang_row), -jnp.sin(ang_col), jnp.sin(ang_col)], axis=-1)
    reps = LANES // HEAD_DIM
    return jnp.tile(cos_h, (1, reps)), jnp.tile(sin_h, (1, reps))


def _band_bias():
    rel = jnp.arange(QBLK + 2 * WINDOW)[None, :] - jnp.arange(QBLK)[:, None]
    return jnp.where((rel >= 0) & (rel <= 2 * WINDOW), 0.0, NEG_INF).astype(F32)


def kernel(x, c, ctx, c_ctx, w_ada, b_ada, w_in, attn_sink, conv_w, conv_b, conv_ln_g, conv_ln_b,
           w_out, post_ln_g, post_ln_b):
    assert w_ada.shape[0] == DEPTH
    b, n, d = x.shape
    cond = jnp.zeros((ADA_ROWS, d), F32).at[:b].set(c).at[b].set(c_ctx)
    mods = _adaln(cond, w_ada[0], b_ada[0][None, :])
    shift, scale, gate = (mods[:b, k * d:(k + 1) * d][:, None, :] for k in range(3))
    shift_c, scale_c = (mods[b:b + 1, k * d:(k + 1) * d] for k in range(2))

    w_in_b = w_in[0].astype(BF16)
    kcT, vc = _ctx_kv(ctx, shift_c, scale_c, w_in_b[:, K_OFF:AG_OFF])
    cos_t, sin_t = _rope_tables(n)
    act, kT, vv = _in_proj(x, shift, scale, cos_t, sin_t, w_in_b)
    conv_w_rep = jnp.broadcast_to(conv_w[0][:, None, :], (CONV_SIZE, SUBLANES, CONV_WIDTH))
    return _mixer(attn_sink[0], act, kT, vv, kcT, vc, _band_bias(),
                  conv_w_rep, conv_b[0][None, :], conv_ln_g[0][None, :], conv_ln_b[0][None, :],
                  w_out[0].astype(BF16), x, gate, post_ln_g[0][None, :], post_ln_b[0][None, :])
```

```python
import functools
import math

import jax
import jax.numpy as jnp
from jax import lax
from jax.experimental import pallas as pl
from jax.experimental.pallas import tpu as pltpu

F32 = jnp.float32
BF16 = jnp.bfloat16

D_MODEL = 1024
GRID_W = 64
HEAD_DIM = 64
N_Q_HEADS = 8
N_KV_HEADS = 2
GQA_GROUP = N_Q_HEADS // N_KV_HEADS
ATTN_WIDTH = N_Q_HEADS * HEAD_DIM
KV_WIDTH = N_KV_HEADS * HEAD_DIM
CONV_WIDTH = D_MODEL - ATTN_WIDTH
CONV_SIZE = 31
CONV_HALF = CONV_SIZE // 2
WINDOW = 128
QBLK = 128
ROPE_BASE = 10000.0
ROT_AXIS_DIM = HEAD_DIM // 2
ROT_HALF = ROT_AXIS_DIM // 2
LN_EPS = 1e-6
NEG_INF = -1e30
DEPTH = 1
ALPHA = (2.0 * DEPTH) ** 0.25

Q_OFF = 0
K_OFF = Q_OFF + ATTN_WIDTH
V_OFF = K_OFF + KV_WIDTH
AG_OFF = V_OFF + KV_WIDTH
CA_OFF = AG_OFF + ATTN_WIDTH
CB_OFF = CA_OFF + CONV_WIDTH
CG_OFF = CB_OFF + CONV_WIDTH
IN_WIDTH = CG_OFF + CONV_WIDTH

ACT_Q = 0
ACT_U = ATTN_WIDTH
ACT_SAG = ACT_U + CONV_WIDTH
ACT_SCG = ACT_SAG + ATTN_WIDTH
ACT_WIDTH = ACT_SCG + CONV_WIDTH

LANES = 128
SUBLANES = 8
CONV_ROWS = 64
SM_ROWS = 32
CONV_COLS = 256
LOG2E = math.log2(math.e)
U_HALO = 16
CONV_WIN_EXTRA = ((U_HALO + CONV_HALF) // SUBLANES) * SUBLANES
TILE_ROWS = 512
ADA_ROWS = 40
ADA_COLS = 512
VMEM_LIMIT = 56 * 1024 * 1024


def _norm_rows(x):
    mu = jnp.mean(x, axis=-1, keepdims=True)
    xc = x - mu
    var = jnp.mean(xc * xc, axis=-1, keepdims=True)
    return xc * lax.rsqrt(var + LN_EPS)


def _silu(x):
    return x * jax.nn.sigmoid(x)


def _adaln_body(c_ref, w_ref, b_ref, o_ref):
    o_ref[...] = jnp.dot(_silu(c_ref[...]), w_ref[...], preferred_element_type=F32) + b_ref[...]


def _adaln(cond, w_ada, b_ada):
    n_cols = w_ada.shape[1]
    return pl.pallas_call(
        _adaln_body,
        grid=(n_cols // ADA_COLS,),
        in_specs=[
            pl.BlockSpec((ADA_ROWS, D_MODEL), lambda j: (0, 0)),
            pl.BlockSpec((D_MODEL, ADA_COLS), lambda j: (0, j)),
            pl.BlockSpec((1, ADA_COLS), lambda j: (0, j)),
        ],
        out_specs=pl.BlockSpec((ADA_ROWS, ADA_COLS), lambda j: (0, j)),
        out_shape=jax.ShapeDtypeStruct((ADA_ROWS, n_cols), F32),
        compiler_params=pltpu.CompilerParams(dimension_semantics=("arbitrary",)),
        name="adaln",
    )(cond, w_ada, b_ada)


def _dup_heads(v):
    lane = lax.broadcasted_iota(jnp.int32, (1, LANES), 1)
    swapped = pltpu.roll(v, HEAD_DIM, 1)
    low = lane < HEAD_DIM
    return jnp.where(low, v, swapped), jnp.where(low, swapped, v)


def _ctx_kv_body(ctx_ref, shift_ref, scale_ref, w_ref, kT_ref, v_ref):
    h = _norm_rows(ctx_ref[...]) * (1.0 + scale_ref[...]) + shift_ref[...]
    kv = jnp.dot(h.astype(BF16), w_ref[...], preferred_element_type=F32)
    kT_ref[...] = kv[:, :KV_WIDTH].T.astype(BF16)
    v0, v1 = _dup_heads(kv[:, KV_WIDTH:])
    v_ref[:, :LANES] = v0.astype(BF16)
    v_ref[:, LANES:] = v1.astype(BF16)


def _ctx_kv(ctx, shift_c, scale_c, w_kv):
    b, m, d = ctx.shape
    return pl.pallas_call(
        _ctx_kv_body,
        grid=(b,),
        in_specs=[
            pl.BlockSpec((None, m, d), lambda i: (i, 0, 0)),
            pl.BlockSpec((1, d), lambda i: (0, 0)),
            pl.BlockSpec((1, d), lambda i: (0, 0)),
            pl.BlockSpec((d, 2 * KV_WIDTH), lambda i: (0, 0)),
        ],
        out_specs=[
            pl.BlockSpec((None, KV_WIDTH, m), lambda i: (i, 0, 0)),
            pl.BlockSpec((None, m, 2 * LANES), lambda i: (i, 0, 0)),
        ],
        out_shape=[
            jax.ShapeDtypeStruct((b, KV_WIDTH, m), BF16),
            jax.ShapeDtypeStruct((b, m, 2 * LANES), BF16),
        ],
        compiler_params=pltpu.CompilerParams(dimension_semantics=("arbitrary",)),
        name="ctx_kv",
    )(ctx, shift_c, scale_c, w_kv)


def _in_proj_body(x_ref, shift_ref, scale_ref, cos_ref, sin_ref, w_ref, act_ref, kT_ref, v_ref):
    h = _norm_rows(x_ref[...]) * (1.0 + scale_ref[...]) + shift_ref[...]
    hb = h.astype(BF16)
    cos = cos_ref[...]
    sin = sin_ref[...]
    lane = lax.broadcasted_iota(jnp.int32, (1, LANES), 1)
    first_half = (lane % ROT_AXIS_DIM) < ROT_HALF

    def rope(t):
        partner = jnp.where(first_half, pltpu.roll(t, LANES - ROT_HALF, 1), pltpu.roll(t, ROT_HALF, 1))
        return t * cos + partner * sin

    def proj(lo, width):
        return jnp.dot(hb, w_ref[:, lo:lo + width], preferred_element_type=F32)

    q = proj(Q_OFF, ATTN_WIDTH)
    for c in range(ATTN_WIDTH // LANES):
        sl = slice(c * LANES, (c + 1) * LANES)
        act_ref[:, ACT_Q + c * LANES:ACT_Q + (c + 1) * LANES] = (rope(q[:, sl]) * (HEAD_DIM ** -0.5 * LOG2E)).astype(BF16)

    kv = proj(K_OFF, 2 * KV_WIDTH)
    kT_ref[...] = rope(kv[:, :KV_WIDTH]).T.astype(BF16)
    v0, v1 = _dup_heads(kv[:, KV_WIDTH:])
    v_ref[:, :LANES] = v0.astype(BF16)
    v_ref[:, LANES:] = v1.astype(BF16)

    act_ref[:, ACT_SAG:ACT_SAG + ATTN_WIDTH] = _silu(proj(AG_OFF, ATTN_WIDTH)).astype(BF16)
    glu = proj(CA_OFF, 2 * CONV_WIDTH)
    act_ref[:, ACT_U:ACT_U + CONV_WIDTH] = (glu[:, :CONV_WIDTH] * jax.nn.sigmoid(glu[:, CONV_WIDTH:])).astype(BF16)
    act_ref[:, ACT_SCG:ACT_SCG + CONV_WIDTH] = _silu(proj(CG_OFF, CONV_WIDTH)).astype(BF16)


def _in_proj(x, shift, scale, cos_t, sin_t, w_in):
    b, n, d = x.shape
    tm = TILE_ROWS
    return pl.pallas_call(
        _in_proj_body,
        grid=(b, n // tm),
        in_specs=[
            pl.BlockSpec((None, tm, d), lambda bi, i: (bi, i, 0)),
            pl.BlockSpec((None, 1, d), lambda bi, i: (bi, 0, 0)),
            pl.BlockSpec((None, 1, d), lambda bi, i: (bi, 0, 0)),
            pl.BlockSpec((tm, LANES), lambda bi, i: (i, 0)),
            pl.BlockSpec((tm, LANES), lambda bi, i: (i, 0)),
            pl.BlockSpec((d, IN_WIDTH), lambda bi, i: (0, 0)),
        ],
        out_specs=[
            pl.BlockSpec((None, tm, ACT_WIDTH), lambda bi, i: (bi, i, 0)),
            pl.BlockSpec((None, KV_WIDTH, tm), lambda bi, i: (bi, 0, i)),
            pl.BlockSpec((None, tm, 2 * LANES), lambda bi, i: (bi, i, 0)),
        ],
        out_shape=[
            jax.ShapeDtypeStruct((b, n, ACT_WIDTH), BF16),
            jax.ShapeDtypeStruct((b, KV_WIDTH, n), BF16),
            jax.ShapeDtypeStruct((b, n, 2 * LANES), BF16),
        ],
        compiler_params=pltpu.CompilerParams(
            dimension_semantics=("arbitrary", "arbitrary"), vmem_limit_bytes=VMEM_LIMIT),
        name="in_proj",
    )(x, shift, scale, cos_t, sin_t, w_in)


def _mixer_body(sink_ref, act_ref, u_prev_ref, u_next_ref,
                kT_ref, kT_prev_ref, kT_next_ref, v_ref, v_prev_ref, v_next_ref,
                kcT_ref, vc_ref, band_ref,
                conv_w_ref, conv_b_ref, cln_g_ref, cln_b_ref, w_out_ref,
                x_ref, gate_ref, pln_g_ref, pln_b_ref,
                o_ref,
                kT4_ref, v4_ref, kcT4_ref, vc4_ref, mixed_ref, ubuf_ref, ushift_ref,
                p_ref, inv_ref, kTw_ref, bias_ref):
    tq = x_ref.shape[0]
    i = pl.program_id(1)
    is_first = i == 0
    is_last = i == pl.num_programs(1) - 1
    n_qblk = tq // QBLK

    for h in range(N_KV_HEADS):
        rows = slice(h * HEAD_DIM, (h + 1) * HEAD_DIM)
        for g in range(GQA_GROUP):
            dst = slice(g * HEAD_DIM, (g + 1) * HEAD_DIM)
            kT4_ref[h, dst, 0:WINDOW] = kT_prev_ref[rows, :]
            kT4_ref[h, dst, WINDOW:WINDOW + tq] = kT_ref[rows, :]
            kT4_ref[h, dst, WINDOW + tq:] = kT_next_ref[rows, :]
            kcT4_ref[h, dst, :] = kcT_ref[rows, :]
        lanes = slice(h * LANES, (h + 1) * LANES)
        for half in range(2):
            dst = slice(half * LANES, (half + 1) * LANES)
            v4_ref[h, 0:WINDOW, dst] = v_prev_ref[:, lanes]
            v4_ref[h, WINDOW:WINDOW + tq, dst] = v_ref[:, lanes]
            v4_ref[h, WINDOW + tq:, dst] = v_next_ref[:, lanes]
            vc4_ref[h, :, dst] = vc_ref[:, lanes]

    def head_lanes(g):
        lane = lax.broadcasted_iota(jnp.int32, (1, GQA_GROUP * HEAD_DIM), 1)
        return (lane >= g * HEAD_DIM) & (lane < (g + 1) * HEAD_DIM)

    n_ctx_tiles = kcT_ref.shape[-1] // LANES
    n_loc_tiles = (QBLK + 2 * WINDOW) // LANES
    loc_cols = n_loc_tiles * LANES
    n_tiles = n_loc_tiles + n_ctx_tiles
    for j in range(n_qblk):
        for h in range(N_KV_HEADS):
            kTw_ref[h, j] = kT4_ref[h, :, j * QBLK:j * QBLK + loc_cols]
        lo = band_ref[:, 0:WINDOW]
        hi = band_ref[:, QBLK + WINDOW:]
        if j == 0:
            lo = lo + jnp.where(is_first, NEG_INF, 0.0)
        if j == n_qblk - 1:
            hi = hi + jnp.where(is_last, NEG_INF, 0.0)
        bias_ref[j, :, 0:WINDOW] = lo
        bias_ref[j, :, WINDOW:] = hi

    def scores(j, q0, h):
        hl = h * GQA_GROUP * HEAD_DIM
        qb = act_ref[pl.ds(q0, QBLK), ACT_Q + hl:ACT_Q + hl + GQA_GROUP * HEAD_DIM]
        lhs = jnp.concatenate(
            [jnp.where(head_lanes(g), qb, jnp.zeros_like(qb)) for g in range(GQA_GROUP)], axis=0)
        return (jnp.dot(lhs, kTw_ref[h, j], preferred_element_type=F32),
                jnp.dot(lhs, kcT4_ref[h], preferred_element_type=F32))

    def softmax(j, h, s_loc, s_ctx):
        for rc in range(0, GQA_GROUP * QBLK, SM_ROWS):
            g, rr = divmod(rc, QBLK)
            sink = sink_ref[h * GQA_GROUP + g] * LOG2E
            tiles = [s_loc[rc:rc + SM_ROWS, t * LANES:(t + 1) * LANES] for t in range(n_loc_tiles)]
            tiles += [s_ctx[rc:rc + SM_ROWS, t * LANES:(t + 1) * LANES] for t in range(n_ctx_tiles)]
            tiles[0] = tiles[0] + bias_ref[j, rr:rr + SM_ROWS, 0:WINDOW]
            tiles[n_loc_tiles - 1] = tiles[n_loc_tiles - 1] + bias_ref[j, rr:rr + SM_ROWS, WINDOW:]
            m = jnp.max(functools.reduce(jnp.maximum, tiles), axis=-1, keepdims=True)
            m = jnp.maximum(m, sink)
            probs = [jnp.exp2(t - m) for t in tiles]
            denom = jnp.sum(functools.reduce(jnp.add, probs), axis=-1, keepdims=True) + jnp.exp2(sink - m)
            inv_ref[h, rc:rc + SM_ROWS, :] = 1.0 / denom
            for t in range(n_tiles):
                p_ref[h, rc:rc + SM_ROWS, t * LANES:(t + 1) * LANES] = probs[t].astype(BF16)

    def values(q0, h):
        hl = slice(h * GQA_GROUP * HEAD_DIM, (h + 1) * GQA_GROUP * HEAD_DIM)
        r = (jnp.dot(p_ref[h, :, 0:loc_cols], v4_ref[h, pl.ds(q0, loc_cols), :], preferred_element_type=F32)
             + jnp.dot(p_ref[h, :, loc_cols:], vc4_ref[h], preferred_element_type=F32))
        r = r * inv_ref[h]
        attn = jnp.zeros((QBLK, GQA_GROUP * HEAD_DIM), F32)
        for g in range(GQA_GROUP):
            attn = jnp.where(head_lanes(g), r[g * QBLK:(g + 1) * QBLK], attn)
        sag = act_ref[pl.ds(q0, QBLK), ACT_SAG + hl.start:ACT_SAG + hl.stop].astype(F32)
        mixed_ref[pl.ds(q0, QBLK), hl] = (attn * sag).astype(BF16)

    u_prev = u_prev_ref[...].astype(F32)
    u_next = u_next_ref[...].astype(F32)
    ubuf_ref[0:U_HALO, :] = jnp.where(is_first, jnp.zeros_like(u_prev), u_prev)
    ubuf_ref[U_HALO:U_HALO + tq, :] = act_ref[:, ACT_U:ACT_U + CONV_WIDTH].astype(F32)
    ubuf_ref[U_HALO + tq:, :] = jnp.where(is_last, jnp.zeros_like(u_next), u_next)
    shift_rows = ushift_ref.shape[1]
    for r in range(1, SUBLANES):
        ushift_ref[r - 1] = ubuf_ref[r:r + shift_rows, :]

    def conv_chunk(r0):
        groups = CONV_ROWS // SUBLANES
        slabs = []
        for cb in range(CONV_WIDTH // CONV_COLS):
            ch = slice(cb * CONV_COLS, (cb + 1) * CONV_COLS)
            acc = jnp.zeros((groups, SUBLANES, CONV_COLS), F32) + conv_b_ref[:, ch]
            for r in range(SUBLANES):
                taps = [t for t in range(CONV_SIZE) if (U_HALO - CONV_HALF + t) % SUBLANES == r]
                src = ubuf_ref if r == 0 else ushift_ref.at[r - 1]
                win = src[pl.ds(r0, CONV_ROWS + CONV_WIN_EXTRA), ch].reshape(-1, SUBLANES, CONV_COLS)
                for t in taps:
                    a = (U_HALO - CONV_HALF + t) // SUBLANES
                    acc = acc + win[a:a + groups] * conv_w_ref[t, :, ch][None]
            slabs.append(acc.reshape(CONV_ROWS, CONV_COLS))
        cn = _silu(_norm_rows(jnp.concatenate(slabs, axis=1)) * cln_g_ref[...] + cln_b_ref[...])
        scg = act_ref[pl.ds(r0, CONV_ROWS), ACT_SCG:ACT_SCG + CONV_WIDTH].astype(F32)
        mixed_ref[pl.ds(r0, CONV_ROWS), ATTN_WIDTH:] = (cn * scg).astype(BF16)

    for j in range(n_qblk):
        q0 = j * QBLK
        logits = [scores(j, q0, h) for h in range(N_KV_HEADS)]
        for c in range(QBLK // CONV_ROWS):
            conv_chunk(q0 + c * CONV_ROWS)
        for h in range(N_KV_HEADS):
            softmax(j, h, *logits[h])
            values(q0, h)

    y = jnp.dot(mixed_ref[...], w_out_ref[...], preferred_element_type=F32)
    z = ALPHA * x_ref[...] + gate_ref[...] * y
    o_ref[...] = _norm_rows(z) * pln_g_ref[...] + pln_b_ref[...]


def _mixer(sink, act, kT, vv, kcT, vc, band, conv_w, conv_b, cln_g, cln_b, w_out, x, gate, pln_g, pln_b):
    b, n, d = x.shape
    tq = TILE_ROWS
    m = kcT.shape[-1]
    qpt = tq // QBLK
    upt = tq // U_HALO
    n_q = n // QBLK
    n_u = n // U_HALO
    u_col = ACT_U // CONV_WIDTH
    const2 = lambda bi, i: (0, 0)
    return pl.pallas_call(
        _mixer_body,
        grid=(b, n // tq),
        in_specs=[
            pl.BlockSpec(memory_space=pltpu.SMEM),
            pl.BlockSpec((None, tq, ACT_WIDTH), lambda bi, i: (bi, i, 0)),
            pl.BlockSpec((None, U_HALO, CONV_WIDTH), lambda bi, i: (bi, jnp.maximum(i * upt - 1, 0), u_col)),
            pl.BlockSpec((None, U_HALO, CONV_WIDTH), lambda bi, i: (bi, jnp.minimum((i + 1) * upt, n_u - 1), u_col)),
            pl.BlockSpec((None, KV_WIDTH, tq), lambda bi, i: (bi, 0, i)),
            pl.BlockSpec((None, KV_WIDTH, WINDOW), lambda bi, i: (bi, 0, jnp.maximum(i * qpt - 1, 0))),
            pl.BlockSpec((None, KV_WIDTH, WINDOW), lambda bi, i: (bi, 0, jnp.minimum((i + 1) * qpt, n_q - 1))),
            pl.BlockSpec((None, tq, 2 * LANES), lambda bi, i: (bi, i, 0)),
            pl.BlockSpec((None, WINDOW, 2 * LANES), lambda bi, i: (bi, jnp.maximum(i * qpt - 1, 0), 0)),
            pl.BlockSpec((None, WINDOW, 2 * LANES), lambda bi, i: (bi, jnp.minimum((i + 1) * qpt, n_q - 1), 0)),
            pl.BlockSpec((None, KV_WIDTH, m), lambda bi, i: (bi, 0, 0)),
            pl.BlockSpec((None, m, 2 * LANES), lambda bi, i: (bi, 0, 0)),
            pl.BlockSpec((QBLK, QBLK + 2 * WINDOW), const2),
            pl.BlockSpec((CONV_SIZE, SUBLANES, CONV_WIDTH), lambda bi, i: (0, 0, 0)),
            pl.BlockSpec((1, CONV_WIDTH), const2),
            pl.BlockSpec((1, CONV_WIDTH), const2),
            pl.BlockSpec((1, CONV_WIDTH), const2),
            pl.BlockSpec((d, d), const2),
            pl.BlockSpec((None, tq, d), lambda bi, i: (bi, i, 0)),
            pl.BlockSpec((None, 1, d), lambda bi, i: (bi, 0, 0)),
            pl.BlockSpec((1, d), const2),
            pl.BlockSpec((1, d), const2),
        ],
        out_specs=pl.BlockSpec((None, tq, d), lambda bi, i: (bi, i, 0)),
        out_shape=jax.ShapeDtypeStruct((b, n, d), F32),
        scratch_shapes=[
            pltpu.VMEM((N_KV_HEADS, GQA_GROUP * HEAD_DIM, tq + 2 * WINDOW), BF16),
            pltpu.VMEM((N_KV_HEADS, tq + 2 * WINDOW, 2 * LANES), BF16),
            pltpu.VMEM((N_KV_HEADS, GQA_GROUP * HEAD_DIM, m), BF16),
            pltpu.VMEM((N_KV_HEADS, m, 2 * LANES), BF16),
            pltpu.VMEM((tq, d), BF16),
            pltpu.VMEM((tq + 2 * U_HALO, CONV_WIDTH), F32),
            pltpu.VMEM((SUBLANES - 1, tq + 2 * U_HALO - SUBLANES, CONV_WIDTH), F32),
            pltpu.VMEM((N_KV_HEADS, GQA_GROUP * QBLK, QBLK + 2 * WINDOW + m), BF16),
            pltpu.VMEM((N_KV_HEADS, GQA_GROUP * QBLK, 1), F32),
            pltpu.VMEM((N_KV_HEADS, tq // QBLK, GQA_GROUP * HEAD_DIM, QBLK + 2 * WINDOW), BF16),
            pltpu.VMEM((tq // QBLK, QBLK, 2 * WINDOW), F32),
        ],
        compiler_params=pltpu.CompilerParams(
            dimension_semantics=("arbitrary", "arbitrary"), vmem_limit_bytes=VMEM_LIMIT),
        name="mixer",
    )(sink, act, act, act, kT, kT, kT, vv, vv, vv, kcT, vc, band,
      conv_w, conv_b, cln_g, cln_b, w_out, x, gate, pln_g, pln_b)


def _rope_tables(n):
    rows = n // GRID_W
    row = jnp.repeat(jnp.arange(rows, dtype=F32), GRID_W)
    colp = jnp.tile(jnp.arange(GRID_W, dtype=F32), rows)
    inv_freq = ROPE_BASE ** (-jnp.arange(0, ROT_AXIS_DIM, 2, dtype=F32) / ROT_AXIS_DIM)
    ang_row = row[:, None] * inv_freq
    ang_col = colp[:, None] * inv_freq
    cos_h = jnp.concatenate([jnp.cos(ang_row), jnp.cos(ang_row), jnp.cos(ang_col), jnp.cos(ang_col)], axis=-1)
    sin_h = jnp.concatenate([-jnp.sin(ang_row), jnp.sin(ang_row), -jnp.sin(ang_col), jnp.sin(ang_col)], axis=-1)
    reps = LANES // HEAD_DIM
    return jnp.tile(cos_h, (1, reps)), jnp.tile(sin_h, (1, reps))


def _band_bias():
    rel = jnp.arange(QBLK + 2 * WINDOW)[None, :] - jnp.arange(QBLK)[:, None]
    return jnp.where((rel >= 0) & (rel <= 2 * WINDOW), 0.0, NEG_INF).astype(F32)


def kernel(x, c, ctx, c_ctx, w_ada, b_ada, w_in, attn_sink, conv_w, conv_b, conv_ln_g, conv_ln_b,
           w_out, post_ln_g, post_ln_b):
    assert w_ada.shape[0] == DEPTH
    b, n, d = x.shape
    cond = jnp.zeros((ADA_ROWS, d), F32).at[:b].set(c).at[b].set(c_ctx)
    mods = _adaln(cond, w_ada[0], b_ada[0][None, :])
    shift, scale, gate = (mods[:b, k * d:(k + 1) * d][:, None, :] for k in range(3))
    shift_c, scale_c = (mods[b:b + 1, k * d:(k + 1) * d] for k in range(2))

    w_in_b = w_in[0].astype(BF16)
    kcT, vc = _ctx_kv(ctx, shift_c, scale_c, w_in_b[:, K_OFF:AG_OFF])
    cos_t, sin_t = _rope_tables(n)
    act, kT, vv = _in_proj(x, shift, scale, cos_t, sin_t, w_in_b)
    conv_w_rep = jnp.broadcast_to(conv_w[0][:, None, :], (CONV_SIZE, SUBLANES, CONV_WIDTH))
    return _mixer(attn_sink[0], act, kT, vv, kcT, vc, _band_bias(),
                  conv_w_rep, conv_b[0][None, :], conv_ln_g[0][None, :], conv_ln_b[0][None, :],
                  w_out[0].astype(BF16), x, gate, post_ln_g[0][None, :], post_ln_b[0][None, :])
```

```python
import functools
import math

import jax
import jax.numpy as jnp
from jax import lax
from jax.experimental import pallas as pl
from jax.experimental.pallas import tpu as pltpu

F32 = jnp.float32
BF16 = jnp.bfloat16

D_MODEL = 1024
GRID_W = 64
HEAD_DIM = 64
N_Q_HEADS = 8
N_KV_HEADS = 2
GQA_GROUP = N_Q_HEADS // N_KV_HEADS
ATTN_WIDTH = N_Q_HEADS * HEAD_DIM
KV_WIDTH = N_KV_HEADS * HEAD_DIM
CONV_WIDTH = D_MODEL - ATTN_WIDTH
CONV_SIZE = 31
CONV_HALF = CONV_SIZE // 2
WINDOW = 128
QBLK = 128
ROPE_BASE = 10000.0
ROT_AXIS_DIM = HEAD_DIM // 2
ROT_HALF = ROT_AXIS_DIM // 2
LN_EPS = 1e-6
NEG_INF = -1e30
DEPTH = 1
ALPHA = (2.0 * DEPTH) ** 0.25

Q_OFF = 0
K_OFF = Q_OFF + ATTN_WIDTH
V_OFF = K_OFF + KV_WIDTH
AG_OFF = V_OFF + KV_WIDTH
CA_OFF = AG_OFF + ATTN_WIDTH
CB_OFF = CA_OFF + CONV_WIDTH
CG_OFF = CB_OFF + CONV_WIDTH
IN_WIDTH = CG_OFF + CONV_WIDTH

ACT_Q = 0
ACT_C = ATTN_WIDTH
ACT_SAG = ACT_C + CONV_WIDTH
ACT_WIDTH = ACT_SAG + ATTN_WIDTH

LANES = 128
SUBLANES = 8
CONV_ROWS = 64
CONV_COLS = 256
SM_ROWS = 32
CONV_PER_PROJ = 4
LOG2E = math.log2(math.e)
U_HALO = 16
CONV_WIN_EXTRA = ((U_HALO + CONV_HALF) // SUBLANES) * SUBLANES
TILE_ROWS = 512
ADA_ROWS = 40
ADA_COLS = 512
VMEM_LIMIT = 56 * 1024 * 1024


def _norm_rows(x):
    mu = jnp.mean(x, axis=-1, keepdims=True)
    xc = x - mu
    var = jnp.mean(xc * xc, axis=-1, keepdims=True)
    return xc * lax.rsqrt(var + LN_EPS)


def _sigmoid(x):
    return 0.5 + 0.5 * jnp.tanh(0.5 * x)


def _silu(x):
    hx = 0.5 * x
    return hx + hx * jnp.tanh(hx)


def _adaln_body(c_ref, w_ref, b_ref, o_ref):
    o_ref[...] = jnp.dot(_silu(c_ref[...]), w_ref[...], preferred_element_type=F32) + b_ref[...]


def _adaln(cond, w_ada, b_ada):
    n_cols = w_ada.shape[1]
    return pl.pallas_call(
        _adaln_body,
        grid=(n_cols // ADA_COLS,),
        in_specs=[
            pl.BlockSpec((ADA_ROWS, D_MODEL), lambda j: (0, 0)),
            pl.BlockSpec((D_MODEL, ADA_COLS), lambda j: (0, j)),
            pl.BlockSpec((1, ADA_COLS), lambda j: (0, j)),
        ],
        out_specs=pl.BlockSpec((ADA_ROWS, ADA_COLS), lambda j: (0, j)),
        out_shape=jax.ShapeDtypeStruct((ADA_ROWS, n_cols), F32),
        compiler_params=pltpu.CompilerParams(dimension_semantics=("arbitrary",)),
        name="adaln",
    )(cond, w_ada, b_ada)


def _dup_heads(v):
    lane = lax.broadcasted_iota(jnp.int32, (1, LANES), 1)
    swapped = pltpu.roll(v, HEAD_DIM, 1)
    low = lane < HEAD_DIM
    return jnp.where(low, v, swapped), jnp.where(low, swapped, v)


def _ctx_kv_body(ctx_ref, shift_ref, scale_ref, w_ref, kT_ref, v_ref):
    h = _norm_rows(ctx_ref[...]) * (1.0 + scale_ref[...]) + shift_ref[...]
    kv = jnp.dot(h.astype(BF16), w_ref[...], preferred_element_type=F32)
    kT_ref[...] = kv[:, :KV_WIDTH].T.astype(BF16)
    v0, v1 = _dup_heads(kv[:, KV_WIDTH:])
    v_ref[:, :LANES] = v0.astype(BF16)
    v_ref[:, LANES:] = v1.astype(BF16)


def _ctx_kv(ctx, shift_c, scale_c, w_kv):
    b, m, d = ctx.shape
    return pl.pallas_call(
        _ctx_kv_body,
        grid=(b,),
        in_specs=[
            pl.BlockSpec((None, m, d), lambda i: (i, 0, 0)),
            pl.BlockSpec((1, d), lambda i: (0, 0)),
            pl.BlockSpec((1, d), lambda i: (0, 0)),
            pl.BlockSpec((d, 2 * KV_WIDTH), lambda i: (0, 0)),
        ],
        out_specs=[
            pl.BlockSpec((None, KV_WIDTH, m), lambda i: (i, 0, 0)),
            pl.BlockSpec((None, m, 2 * LANES), lambda i: (i, 0, 0)),
        ],
        out_shape=[
            jax.ShapeDtypeStruct((b, KV_WIDTH, m), BF16),
            jax.ShapeDtypeStruct((b, m, 2 * LANES), BF16),
        ],
        compiler_params=pltpu.CompilerParams(dimension_semantics=("arbitrary",)),
        name="ctx_kv",
    )(ctx, shift_c, scale_c, w_kv)


def _in_proj_body(x_ref, x_prev_ref, x_next_ref, shift_ref, scale_ref, cos_ref, sin_ref, w_ref,
                  conv_w_ref, conv_b_ref, cln_g_ref, cln_b_ref,
                  act_ref, kT_ref, v_ref,
                  hbuf_ref, ubuf_ref, ushift_ref, scg_ref):
    tm = x_ref.shape[0]
    i = pl.program_id(1)
    is_first = i == 0
    is_last = i == pl.num_programs(1) - 1

    def modulate(xv):
        return (_norm_rows(xv) * (1.0 + scale_ref[...]) + shift_ref[...]).astype(BF16)

    hbuf_ref[0:U_HALO, :] = modulate(x_prev_ref[...])
    hbuf_ref[U_HALO:U_HALO + tm, :] = modulate(x_ref[...])
    hbuf_ref[U_HALO + tm:, :] = modulate(x_next_ref[...])
    main = slice(U_HALO, U_HALO + tm)

    cos = cos_ref[...]
    sin = sin_ref[...]
    lane = lax.broadcasted_iota(jnp.int32, (1, LANES), 1)
    first_half = (lane % ROT_AXIS_DIM) < ROT_HALF

    def rope(t):
        partner = jnp.where(first_half, pltpu.roll(t, LANES - ROT_HALF, 1), pltpu.roll(t, ROT_HALF, 1))
        return t * cos + partner * sin

    def proj(rows, lo, width):
        return jnp.dot(hbuf_ref[rows, :], w_ref[:, lo:lo + width], preferred_element_type=F32)

    glu = proj(slice(None), CA_OFF, 2 * CONV_WIDTH)
    u = glu[:, :CONV_WIDTH] * _sigmoid(glu[:, CONV_WIDTH:])
    ubuf_ref[0:U_HALO, :] = jnp.where(is_first, 0.0, u[0:U_HALO])
    ubuf_ref[U_HALO:U_HALO + tm, :] = u[U_HALO:U_HALO + tm]
    ubuf_ref[U_HALO + tm:, :] = jnp.where(is_last, 0.0, u[U_HALO + tm:])
    shift_rows = ushift_ref.shape[1]
    for r in range(1, SUBLANES):
        ushift_ref[r - 1] = ubuf_ref[r:r + shift_rows, :]
    scg_ref[...] = _silu(proj(main, CG_OFF, CONV_WIDTH))

    def projection_work():
        for c2 in range(ATTN_WIDTH // (2 * LANES)):
            q = proj(main, Q_OFF + c2 * 2 * LANES, 2 * LANES)
            for c in range(2):
                dst = ACT_Q + (2 * c2 + c) * LANES
                act_ref[:, dst:dst + LANES] = (
                    rope(q[:, c * LANES:(c + 1) * LANES]) * (HEAD_DIM ** -0.5 * LOG2E)).astype(BF16)
            yield
        kv = proj(main, K_OFF, 2 * KV_WIDTH)
        kT_ref[...] = rope(kv[:, :KV_WIDTH]).T.astype(BF16)
        v0, v1 = _dup_heads(kv[:, KV_WIDTH:])
        v_ref[:, :LANES] = v0.astype(BF16)
        v_ref[:, LANES:] = v1.astype(BF16)
        yield
        for c2 in range(ATTN_WIDTH // (2 * LANES)):
            lo = c2 * 2 * LANES
            act_ref[:, ACT_SAG + lo:ACT_SAG + lo + 2 * LANES] = _silu(proj(main, AG_OFF + lo, 2 * LANES)).astype(BF16)
            yield

    def conv_work():
        groups = CONV_ROWS // SUBLANES
        for r0 in range(0, tm, CONV_ROWS):
            slabs = []
            for cb in range(CONV_WIDTH // CONV_COLS):
                ch = slice(cb * CONV_COLS, (cb + 1) * CONV_COLS)
                acc = jnp.zeros((groups, SUBLANES, CONV_COLS), F32) + conv_b_ref[:, ch]
                for r in range(SUBLANES):
                    taps = [t for t in range(CONV_SIZE) if (U_HALO - CONV_HALF + t) % SUBLANES == r]
                    src = ubuf_ref if r == 0 else ushift_ref.at[r - 1]
                    win = src[r0:r0 + CONV_ROWS + CONV_WIN_EXTRA, ch].reshape(-1, SUBLANES, CONV_COLS)
                    for t in taps:
                        a = (U_HALO - CONV_HALF + t) // SUBLANES
                        acc = acc + win[a:a + groups] * conv_w_ref[t, :, ch][None]
                slabs.append(acc.reshape(CONV_ROWS, CONV_COLS))
                yield
            cn = _silu(_norm_rows(jnp.concatenate(slabs, axis=1)) * cln_g_ref[...] + cln_b_ref[...])
            act_ref[r0:r0 + CONV_ROWS, ACT_C:ACT_C + CONV_WIDTH] = (
                cn * scg_ref[r0:r0 + CONV_ROWS, :]).astype(BF16)
            yield

    conv, projections = conv_work(), projection_work()
    conv_left = proj_left = True
    while conv_left or proj_left:
        for _ in range(CONV_PER_PROJ):
            conv_left = conv_left and next(conv, "done") != "done"
        proj_left = proj_left and next(projections, "done") != "done"


def _in_proj(x, shift, scale, cos_t, sin_t, w_in, conv_w, conv_b, cln_g, cln_b):
    b, n, d = x.shape
    tm = TILE_ROWS
    hpt = tm // U_HALO
    n_h = n // U_HALO
    const2 = lambda bi, i: (0, 0)
    return pl.pallas_call(
        _in_proj_body,
        grid=(b, n // tm),
        in_specs=[
            pl.BlockSpec((None, tm, d), lambda bi, i: (bi, i, 0)),
            pl.BlockSpec((None, U_HALO, d), lambda bi, i: (bi, jnp.maximum(i * hpt - 1, 0), 0)),
            pl.BlockSpec((None, U_HALO, d), lambda bi, i: (bi, jnp.minimum((i + 1) * hpt, n_h - 1), 0)),
            pl.BlockSpec((None, 1, d), lambda bi, i: (bi, 0, 0)),
            pl.BlockSpec((None, 1, d), lambda bi, i: (bi, 0, 0)),
            pl.BlockSpec((tm, LANES), lambda bi, i: (i, 0)),
            pl.BlockSpec((tm, LANES), lambda bi, i: (i, 0)),
            pl.BlockSpec((d, IN_WIDTH), const2),
            pl.BlockSpec((CONV_SIZE, SUBLANES, CONV_WIDTH), lambda bi, i: (0, 0, 0)),
            pl.BlockSpec((1, CONV_WIDTH), const2),
            pl.BlockSpec((1, CONV_WIDTH), const2),
            pl.BlockSpec((1, CONV_WIDTH), const2),
        ],
        out_specs=[
            pl.BlockSpec((None, tm, ACT_WIDTH), lambda bi, i: (bi, i, 0)),
            pl.BlockSpec((None, KV_WIDTH, tm), lambda bi, i: (bi, 0, i)),
            pl.BlockSpec((None, tm, 2 * LANES), lambda bi, i: (bi, i, 0)),
        ],
        out_shape=[
            jax.ShapeDtypeStruct((b, n, ACT_WIDTH), BF16),
            jax.ShapeDtypeStruct((b, KV_WIDTH, n), BF16),
            jax.ShapeDtypeStruct((b, n, 2 * LANES), BF16),
        ],
        scratch_shapes=[
            pltpu.VMEM((tm + 2 * U_HALO, d), BF16),
            pltpu.VMEM((tm + 2 * U_HALO, CONV_WIDTH), F32),
            pltpu.VMEM((SUBLANES - 1, tm + 2 * U_HALO - SUBLANES, CONV_WIDTH), F32),
            pltpu.VMEM((tm, CONV_WIDTH), F32),
        ],
        compiler_params=pltpu.CompilerParams(
            dimension_semantics=("arbitrary", "arbitrary"), vmem_limit_bytes=VMEM_LIMIT),
        name="in_proj",
    )(x, x, x, shift, scale, cos_t, sin_t, w_in, conv_w, conv_b, cln_g, cln_b)


def _mixer_body(sink_ref, act_ref,
                kT_ref, kT_prev_ref, kT_next_ref, v_ref, v_prev_ref, v_next_ref,
                kcT_ref, vc_ref, band_ref, w_out_ref,
                x_ref, gate_ref, pln_g_ref, pln_b_ref,
                o_ref,
                kT4_ref, v4_ref, kcT4_ref, vc4_ref, mixed_ref, p_ref, inv_ref, kTw_ref, bias_ref):
    tq = x_ref.shape[0]
    i = pl.program_id(1)
    is_first = i == 0
    is_last = i == pl.num_programs(1) - 1
    n_qblk = tq // QBLK

    for h in range(N_KV_HEADS):
        rows = slice(h * HEAD_DIM, (h + 1) * HEAD_DIM)
        for g in range(GQA_GROUP):
            dst = slice(g * HEAD_DIM, (g + 1) * HEAD_DIM)
            kT4_ref[h, dst, 0:WINDOW] = kT_prev_ref[rows, :]
            kT4_ref[h, dst, WINDOW:WINDOW + tq] = kT_ref[rows, :]
            kT4_ref[h, dst, WINDOW + tq:] = kT_next_ref[rows, :]
            kcT4_ref[h, dst, :] = kcT_ref[rows, :]
        lanes = slice(h * LANES, (h + 1) * LANES)
        for half in range(2):
            dst = slice(half * LANES, (half + 1) * LANES)
            v4_ref[h, 0:WINDOW, dst] = v_prev_ref[:, lanes]
            v4_ref[h, WINDOW:WINDOW + tq, dst] = v_ref[:, lanes]
            v4_ref[h, WINDOW + tq:, dst] = v_next_ref[:, lanes]
            vc4_ref[h, :, dst] = vc_ref[:, lanes]

    def head_lanes(g):
        lane = lax.broadcasted_iota(jnp.int32, (1, GQA_GROUP * HEAD_DIM), 1)
        return (lane >= g * HEAD_DIM) & (lane < (g + 1) * HEAD_DIM)

    n_ctx_tiles = kcT_ref.shape[-1] // LANES
    n_loc_tiles = (QBLK + 2 * WINDOW) // LANES
    loc_cols = n_loc_tiles * LANES
    n_tiles = n_loc_tiles + n_ctx_tiles
    for j in range(n_qblk):
        for h in range(N_KV_HEADS):
            kTw_ref[h, j] = kT4_ref[h, :, j * QBLK:j * QBLK + loc_cols]
        lo = band_ref[:, 0:WINDOW]
        hi = band_ref[:, QBLK + WINDOW:]
        if j == 0:
            lo = lo + jnp.where(is_first, NEG_INF, 0.0)
        if j == n_qblk - 1:
            hi = hi + jnp.where(is_last, NEG_INF, 0.0)
        bias_ref[j, :, 0:WINDOW] = lo
        bias_ref[j, :, WINDOW:] = hi

    mixed_ref[:, ATTN_WIDTH:] = act_ref[:, ACT_C:ACT_C + CONV_WIDTH]

    for j in range(n_qblk):
        q0 = j * QBLK
        for h in range(N_KV_HEADS):
            hl = slice(h * GQA_GROUP * HEAD_DIM, (h + 1) * GQA_GROUP * HEAD_DIM)
            qb = act_ref[q0:q0 + QBLK, ACT_Q + hl.start:ACT_Q + hl.stop]
            lhs = jnp.concatenate(
                [jnp.where(head_lanes(g), qb, jnp.zeros_like(qb)) for g in range(GQA_GROUP)], axis=0)
            s_loc = jnp.dot(lhs, kTw_ref[h, j], preferred_element_type=F32)
            s_ctx = jnp.dot(lhs, kcT4_ref[h], preferred_element_type=F32)
            for rc in range(0, GQA_GROUP * QBLK, SM_ROWS):
                g, rr = divmod(rc, QBLK)
                sink = sink_ref[h * GQA_GROUP + g] * LOG2E
                tiles = [s_loc[rc:rc + SM_ROWS, t * LANES:(t + 1) * LANES] for t in range(n_loc_tiles)]
                tiles += [s_ctx[rc:rc + SM_ROWS, t * LANES:(t + 1) * LANES] for t in range(n_ctx_tiles)]
                tiles[0] = tiles[0] + bias_ref[j, rr:rr + SM_ROWS, 0:WINDOW]
                tiles[n_loc_tiles - 1] = tiles[n_loc_tiles - 1] + bias_ref[j, rr:rr + SM_ROWS, WINDOW:]
                m = jnp.max(functools.reduce(jnp.maximum, tiles), axis=-1, keepdims=True)
                m = jnp.maximum(m, sink)
                probs = [jnp.exp2(t - m) for t in tiles]
                denom = (jnp.sum(functools.reduce(jnp.add, probs), axis=-1, keepdims=True)
                         + jnp.exp2(sink - m))
                inv_ref[h, rc:rc + SM_ROWS, :] = 1.0 / denom
                for t in range(n_tiles):
                    p_ref[h, rc:rc + SM_ROWS, t * LANES:(t + 1) * LANES] = probs[t].astype(BF16)
            r = (jnp.dot(p_ref[h, :, 0:loc_cols], v4_ref[h, q0:q0 + loc_cols, :], preferred_element_type=F32)
                 + jnp.dot(p_ref[h, :, loc_cols:], vc4_ref[h], preferred_element_type=F32))
            r = r * inv_ref[h]
            attn = jnp.zeros((QBLK, GQA_GROUP * HEAD_DIM), F32)
            for g in range(GQA_GROUP):
                attn = jnp.where(head_lanes(g), r[g * QBLK:(g + 1) * QBLK], attn)
            sag = act_ref[q0:q0 + QBLK, ACT_SAG + hl.start:ACT_SAG + hl.stop].astype(F32)
            mixed_ref[q0:q0 + QBLK, hl] = (attn * sag).astype(BF16)

    y = jnp.dot(mixed_ref[...], w_out_ref[...], preferred_element_type=F32)
    z = ALPHA * x_ref[...] + gate_ref[...] * y
    o_ref[...] = _norm_rows(z) * pln_g_ref[...] + pln_b_ref[...]


def _mixer(sink, act, kT, vv, kcT, vc, band, w_out, x, gate, pln_g, pln_b):
    b, n, d = x.shape
    tq = TILE_ROWS
    m = kcT.shape[-1]
    qpt = tq // QBLK
    n_q = n // QBLK
    const2 = lambda bi, i: (0, 0)
    return pl.pallas_call(
        _mixer_body,
        grid=(b, n // tq),
        in_specs=[
            pl.BlockSpec(memory_space=pltpu.SMEM),
            pl.BlockSpec((None, tq, ACT_WIDTH), lambda bi, i: (bi, i, 0)),
            pl.BlockSpec((None, KV_WIDTH, tq), lambda bi, i: (bi, 0, i)),
            pl.BlockSpec((None, KV_WIDTH, WINDOW), lambda bi, i: (bi, 0, jnp.maximum(i * qpt - 1, 0))),
            pl.BlockSpec((None, KV_WIDTH, WINDOW), lambda bi, i: (bi, 0, jnp.minimum((i + 1) * qpt, n_q - 1))),
            pl.BlockSpec((None, tq, 2 * LANES), lambda bi, i: (bi, i, 0)),
            pl.BlockSpec((None, WINDOW, 2 * LANES), lambda bi, i: (bi, jnp.maximum(i * qpt - 1, 0), 0)),
            pl.BlockSpec((None, WINDOW, 2 * LANES), lambda bi, i: (bi, jnp.minimum((i + 1) * qpt, n_q - 1), 0)),
            pl.BlockSpec((None, KV_WIDTH, m), lambda bi, i: (bi, 0, 0)),
            pl.BlockSpec((None, m, 2 * LANES), lambda bi, i: (bi, 0, 0)),
            pl.BlockSpec((QBLK, QBLK + 2 * WINDOW), const2),
            pl.BlockSpec((d, d), const2),
            pl.BlockSpec((None, tq, d), lambda bi, i: (bi, i, 0)),
            pl.BlockSpec((None, 1, d), lambda bi, i: (bi, 0, 0)),
            pl.BlockSpec((1, d), const2),
            pl.BlockSpec((1, d), const2),
        ],
        out_specs=pl.BlockSpec((None, tq, d), lambda bi, i: (bi, i, 0)),
        out_shape=jax.ShapeDtypeStruct((b, n, d), F32),
        scratch_shapes=[
            pltpu.VMEM((N_KV_HEADS, GQA_GROUP * HEAD_DIM, tq + 2 * WINDOW), BF16),
            pltpu.VMEM((N_KV_HEADS, tq + 2 * WINDOW, 2 * LANES), BF16),
            pltpu.VMEM((N_KV_HEADS, GQA_GROUP * HEAD_DIM, m), BF16),
            pltpu.VMEM((N_KV_HEADS, m, 2 * LANES), BF16),
            pltpu.VMEM((tq, d), BF16),
            pltpu.VMEM((N_KV_HEADS, GQA_GROUP * QBLK, QBLK + 2 * WINDOW + m), BF16),
            pltpu.VMEM((N_KV_HEADS, GQA_GROUP * QBLK, 1), F32),
            pltpu.VMEM((N_KV_HEADS, tq // QBLK, GQA_GROUP * HEAD_DIM, QBLK + 2 * WINDOW), BF16),
            pltpu.VMEM((tq // QBLK, QBLK, 2 * WINDOW), F32),
        ],
        compiler_params=pltpu.CompilerParams(
            dimension_semantics=("arbitrary", "arbitrary"), vmem_limit_bytes=VMEM_LIMIT),
        name="mixer",
    )(sink, act, kT, kT, kT, vv, vv, vv, kcT, vc, band, w_out, x, gate, pln_g, pln_b)


def _rope_tables(n):
    rows = n // GRID_W
    row = jnp.repeat(jnp.arange(rows, dtype=F32), GRID_W)
    colp = jnp.tile(jnp.arange(GRID_W, dtype=F32), rows)
    inv_freq = ROPE_BASE ** (-jnp.arange(0, ROT_AXIS_DIM, 2, dtype=F32) / ROT_AXIS_DIM)
    ang_row = row[:, None] * inv_freq
    ang_col = colp[:, None] * inv_freq
    cos_h = jnp.concatenate([jnp.cos(ang_row), jnp.cos(ang_row), jnp.cos(ang_col), jnp.cos(ang_col)], axis=-1)
    sin_h = jnp.concatenate([-jnp.sin(ang_row), jnp.sin(ang_row), -jnp.sin(ang_col), jnp.sin(ang_col)], axis=-1)
    reps = LANES // HEAD_DIM
    return jnp.tile(cos_h, (1, reps)), jnp.tile(sin_h, (1, reps))


def _band_bias():
    rel = jnp.arange(QBLK + 2 * WINDOW)[None, :] - jnp.arange(QBLK)[:, None]
    return jnp.where((rel >= 0) & (rel <= 2 * WINDOW), 0.0, NEG_INF).astype(F32)


def kernel(x, c, ctx, c_ctx, w_ada, b_ada, w_in, attn_sink, conv_w, conv_b, conv_ln_g, conv_ln_b,
           w_out, post_ln_g, post_ln_b):
    assert w_ada.shape[0] == DEPTH
    b, n, d = x.shape
    cond = jnp.zeros((ADA_ROWS, d), F32).at[:b].set(c).at[b].set(c_ctx)
    mods = _adaln(cond, w_ada[0], b_ada[0][None, :])
    shift, scale, gate = (mods[:b, k * d:(k + 1) * d][:, None, :] for k in range(3))
    shift_c, scale_c = (mods[b:b + 1, k * d:(k + 1) * d] for k in range(2))

    w_in_b = w_in[0].astype(BF16)
    kcT, vc = _ctx_kv(ctx, shift_c, scale_c, w_in_b[:, K_OFF:AG_OFF])
    cos_t, sin_t = _rope_tables(n)
    conv_w_rep = jnp.broadcast_to(conv_w[0][:, None, :], (CONV_SIZE, SUBLANES, CONV_WIDTH))
    act, kT, vv = _in_proj(x, shift, scale, cos_t, sin_t, w_in_b,
                           conv_w_rep, conv_b[0][None, :], conv_ln_g[0][None, :], conv_ln_b[0][None, :])
    return _mixer(attn_sink[0], act, kT, vv, kcT, vc, _band_bias(),
                  w_out[0].astype(BF16), x, gate, post_ln_g[0][None, :], post_ln_b[0][None, :])
```

```python
import functools
import math

import jax
import jax.numpy as jnp
from jax import lax
from jax.experimental import pallas as pl
from jax.experimental.pallas import tpu as pltpu

F32 = jnp.float32
BF16 = jnp.bfloat16

D_MODEL = 1024
GRID_W = 64
HEAD_DIM = 64
N_Q_HEADS = 8
N_KV_HEADS = 2
GQA_GROUP = N_Q_HEADS // N_KV_HEADS
ATTN_WIDTH = N_Q_HEADS * HEAD_DIM
KV_WIDTH = N_KV_HEADS * HEAD_DIM
CONV_WIDTH = D_MODEL - ATTN_WIDTH
CONV_SIZE = 31
CONV_HALF = CONV_SIZE // 2
WINDOW = 128
QBLK = 128
ROPE_BASE = 10000.0
ROT_AXIS_DIM = HEAD_DIM // 2
ROT_HALF = ROT_AXIS_DIM // 2
LN_EPS = 1e-6
NEG_INF = -1e30
DEPTH = 1
ALPHA = (2.0 * DEPTH) ** 0.25

Q_OFF = 0
K_OFF = Q_OFF + ATTN_WIDTH
V_OFF = K_OFF + KV_WIDTH
AG_OFF = V_OFF + KV_WIDTH
CA_OFF = AG_OFF + ATTN_WIDTH
CB_OFF = CA_OFF + CONV_WIDTH
CG_OFF = CB_OFF + CONV_WIDTH
IN_WIDTH = CG_OFF + CONV_WIDTH

ACT_Q = 0
ACT_C = ATTN_WIDTH
ACT_SAG = ACT_C + CONV_WIDTH
ACT_WIDTH = ACT_SAG + ATTN_WIDTH

LANES = 128
SUBLANES = 8
CONV_ROWS = 64
CONV_COLS = 256
SM_ROWS = 32
CONV_PER_PROJ = 4
LOG2E = math.log2(math.e)
U_HALO = 16
CONV_WIN_EXTRA = ((U_HALO + CONV_HALF) // SUBLANES) * SUBLANES
TILE_ROWS = 512
ADA_ROWS = 40
ADA_COLS = 512
VMEM_LIMIT = 56 * 1024 * 1024


def _norm_rows(x, eps=LN_EPS):
    mu = jnp.mean(x, axis=-1, keepdims=True)
    xc = x - mu
    var = jnp.mean(xc * xc, axis=-1, keepdims=True)
    return xc * lax.rsqrt(var + eps)


def _sigmoid(x):
    return 0.5 + 0.5 * jnp.tanh(0.5 * x)


def _silu(x):
    hx = 0.5 * x
    return hx + hx * jnp.tanh(hx)


def _adaln_body(c_ref, w_ref, b_ref, o_ref):
    o_ref[...] = jnp.dot(_silu(c_ref[...]), w_ref[...], preferred_element_type=F32) + b_ref[...]


def _adaln(cond, w_ada, b_ada):
    n_cols = w_ada.shape[1]
    return pl.pallas_call(
        _adaln_body,
        grid=(n_cols // ADA_COLS,),
        in_specs=[
            pl.BlockSpec((ADA_ROWS, D_MODEL), lambda j: (0, 0)),
            pl.BlockSpec((D_MODEL, ADA_COLS), lambda j: (0, j)),
            pl.BlockSpec((1, ADA_COLS), lambda j: (0, j)),
        ],
        out_specs=pl.BlockSpec((ADA_ROWS, ADA_COLS), lambda j: (0, j)),
        out_shape=jax.ShapeDtypeStruct((ADA_ROWS, n_cols), F32),
        compiler_params=pltpu.CompilerParams(dimension_semantics=("arbitrary",)),
        name="adaln",
    )(cond, w_ada, b_ada)


def _dup_heads(v):
    lane = lax.broadcasted_iota(jnp.int32, (1, LANES), 1)
    swapped = pltpu.roll(v, HEAD_DIM, 1)
    low = lane < HEAD_DIM
    return jnp.where(low, v, swapped), jnp.where(low, swapped, v)


def _ctx_kv_body(ctx_ref, shift_ref, scale_ref, w_ref, kT_ref, v_ref):
    h = _norm_rows(ctx_ref[...]) * (1.0 + scale_ref[...]) + shift_ref[...]
    kv = jnp.dot(h.astype(BF16), w_ref[...], preferred_element_type=F32)
    kT_ref[...] = kv[:, :KV_WIDTH].T.astype(BF16)
    v0, v1 = _dup_heads(kv[:, KV_WIDTH:])
    v_ref[:, :LANES] = v0.astype(BF16)
    v_ref[:, LANES:] = v1.astype(BF16)


def _ctx_kv(ctx, shift_c, scale_c, w_kv):
    b, m, d = ctx.shape
    return pl.pallas_call(
        _ctx_kv_body,
        grid=(b,),
        in_specs=[
            pl.BlockSpec((None, m, d), lambda i: (i, 0, 0)),
            pl.BlockSpec((1, d), lambda i: (0, 0)),
            pl.BlockSpec((1, d), lambda i: (0, 0)),
            pl.BlockSpec((d, 2 * KV_WIDTH), lambda i: (0, 0)),
        ],
        out_specs=[
            pl.BlockSpec((None, KV_WIDTH, m), lambda i: (i, 0, 0)),
            pl.BlockSpec((None, m, 2 * LANES), lambda i: (i, 0, 0)),
        ],
        out_shape=[
            jax.ShapeDtypeStruct((b, KV_WIDTH, m), BF16),
            jax.ShapeDtypeStruct((b, m, 2 * LANES), BF16),
        ],
        compiler_params=pltpu.CompilerParams(dimension_semantics=("arbitrary",)),
        name="ctx_kv",
    )(ctx, shift_c, scale_c, w_kv)


def _in_proj_body(x_ref, x_prev_ref, x_next_ref, shift_ref, scale_ref, cos_ref, sin_ref, w_ref,
                  conv_w_ref, conv_b_ref, cln_g_ref, cln_b_ref,
                  act_ref, kT_ref, v_ref,
                  hbuf_ref, ubuf_ref, ushift_ref, scg_ref):
    tm = x_ref.shape[0]
    i = pl.program_id(1)
    is_first = i == 0
    is_last = i == pl.num_programs(1) - 1

    def modulate(xv):
        return (_norm_rows(xv) * (1.0 + scale_ref[...]) + shift_ref[...]).astype(BF16)

    hbuf_ref[0:U_HALO, :] = modulate(x_prev_ref[...])
    hbuf_ref[U_HALO:U_HALO + tm, :] = modulate(x_ref[...])
    hbuf_ref[U_HALO + tm:, :] = modulate(x_next_ref[...])
    main = slice(U_HALO, U_HALO + tm)

    cos = cos_ref[...]
    sin = sin_ref[...]
    lane = lax.broadcasted_iota(jnp.int32, (1, LANES), 1)
    first_half = (lane % ROT_AXIS_DIM) < ROT_HALF

    def rope(t):
        partner = jnp.where(first_half, pltpu.roll(t, LANES - ROT_HALF, 1), pltpu.roll(t, ROT_HALF, 1))
        return t * cos + partner * sin

    def proj(rows, lo, width):
        return jnp.dot(hbuf_ref[rows, :], w_ref[:, lo:lo + width], preferred_element_type=F32)

    glu = proj(slice(None), CA_OFF, 2 * CONV_WIDTH)
    u = glu[:, :CONV_WIDTH] * _sigmoid(glu[:, CONV_WIDTH:])
    ubuf_ref[0:U_HALO, :] = jnp.where(is_first, 0.0, u[0:U_HALO])
    ubuf_ref[U_HALO:U_HALO + tm, :] = u[U_HALO:U_HALO + tm]
    ubuf_ref[U_HALO + tm:, :] = jnp.where(is_last, 0.0, u[U_HALO + tm:])
    shift_rows = ushift_ref.shape[1]
    for r in range(1, SUBLANES):
        ushift_ref[r - 1] = ubuf_ref[r:r + shift_rows, :]
    scg_ref[...] = _silu(proj(main, CG_OFF, CONV_WIDTH))

    def projection_work():
        for c2 in range(ATTN_WIDTH // (2 * LANES)):
            q = proj(main, Q_OFF + c2 * 2 * LANES, 2 * LANES)
            for c in range(2):
                dst = ACT_Q + (2 * c2 + c) * LANES
                act_ref[:, dst:dst + LANES] = (
                    rope(q[:, c * LANES:(c + 1) * LANES]) * (HEAD_DIM ** -0.5 * LOG2E)).astype(BF16)
            yield
        kv = proj(main, K_OFF, 2 * KV_WIDTH)
        kT_ref[...] = rope(kv[:, :KV_WIDTH]).T.astype(BF16)
        v0, v1 = _dup_heads(kv[:, KV_WIDTH:])
        v_ref[:, :LANES] = v0.astype(BF16)
        v_ref[:, LANES:] = v1.astype(BF16)
        yield
        for c2 in range(ATTN_WIDTH // (2 * LANES)):
            lo = c2 * 2 * LANES
            act_ref[:, ACT_SAG + lo:ACT_SAG + lo + 2 * LANES] = _silu(proj(main, AG_OFF + lo, 2 * LANES)).astype(BF16)
            yield

    def conv_work():
        groups = CONV_ROWS // SUBLANES
        for r0 in range(0, tm, CONV_ROWS):
            slabs = []
            for cb in range(CONV_WIDTH // CONV_COLS):
                ch = slice(cb * CONV_COLS, (cb + 1) * CONV_COLS)
                acc = jnp.zeros((groups, SUBLANES, CONV_COLS), F32) + conv_b_ref[:, ch]
                for r in range(SUBLANES):
                    taps = [t for t in range(CONV_SIZE) if (U_HALO - CONV_HALF + t) % SUBLANES == r]
                    src = ubuf_ref if r == 0 else ushift_ref.at[r - 1]
                    win = src[r0:r0 + CONV_ROWS + CONV_WIN_EXTRA, ch].reshape(-1, SUBLANES, CONV_COLS)
                    for t in taps:
                        a = (U_HALO - CONV_HALF + t) // SUBLANES
                        acc = acc + win[a:a + groups] * conv_w_ref[t, :, ch][None]
                slabs.append(acc.reshape(CONV_ROWS, CONV_COLS))
                yield
            cn = _silu(_norm_rows(jnp.concatenate(slabs, axis=1)) * cln_g_ref[...] + cln_b_ref[...])
            act_ref[r0:r0 + CONV_ROWS, ACT_C:ACT_C + CONV_WIDTH] = (
                cn * scg_ref[r0:r0 + CONV_ROWS, :]).astype(BF16)
            yield

    conv, projections = conv_work(), projection_work()
    conv_left = proj_left = True
    while conv_left or proj_left:
        for _ in range(CONV_PER_PROJ):
            conv_left = conv_left and next(conv, "done") != "done"
        proj_left = proj_left and next(projections, "done") != "done"


def _in_proj(x, shift, scale, cos_t, sin_t, w_in, conv_w, conv_b, cln_g, cln_b):
    b, n, d = x.shape
    tm = TILE_ROWS
    hpt = tm // U_HALO
    n_h = n // U_HALO
    const2 = lambda bi, i: (0, 0)
    return pl.pallas_call(
        _in_proj_body,
        grid=(b, n // tm),
        in_specs=[
            pl.BlockSpec((None, tm, d), lambda bi, i: (bi, i, 0)),
            pl.BlockSpec((None, U_HALO, d), lambda bi, i: (bi, jnp.maximum(i * hpt - 1, 0), 0)),
            pl.BlockSpec((None, U_HALO, d), lambda bi, i: (bi, jnp.minimum((i + 1) * hpt, n_h - 1), 0)),
            pl.BlockSpec((None, 1, d), lambda bi, i: (bi, 0, 0)),
            pl.BlockSpec((None, 1, d), lambda bi, i: (bi, 0, 0)),
            pl.BlockSpec((tm, LANES), lambda bi, i: (i, 0)),
            pl.BlockSpec((tm, LANES), lambda bi, i: (i, 0)),
            pl.BlockSpec((d, IN_WIDTH), const2),
            pl.BlockSpec((CONV_SIZE, SUBLANES, CONV_WIDTH), lambda bi, i: (0, 0, 0)),
            pl.BlockSpec((1, CONV_WIDTH), const2),
            pl.BlockSpec((1, CONV_WIDTH), const2),
            pl.BlockSpec((1, CONV_WIDTH), const2),
        ],
        out_specs=[
            pl.BlockSpec((None, tm, ACT_WIDTH), lambda bi, i: (bi, i, 0)),
            pl.BlockSpec((None, KV_WIDTH, tm), lambda bi, i: (bi, 0, i)),
            pl.BlockSpec((None, tm, 2 * LANES), lambda bi, i: (bi, i, 0)),
        ],
        out_shape=[
            jax.ShapeDtypeStruct((b, n, ACT_WIDTH), BF16),
            jax.ShapeDtypeStruct((b, KV_WIDTH, n), BF16),
            jax.ShapeDtypeStruct((b, n, 2 * LANES), BF16),
        ],
        scratch_shapes=[
            pltpu.VMEM((tm + 2 * U_HALO, d), BF16),
            pltpu.VMEM((tm + 2 * U_HALO, CONV_WIDTH), F32),
            pltpu.VMEM((SUBLANES - 1, tm + 2 * U_HALO - SUBLANES, CONV_WIDTH), F32),
            pltpu.VMEM((tm, CONV_WIDTH), F32),
        ],
        compiler_params=pltpu.CompilerParams(
            dimension_semantics=("arbitrary", "arbitrary"), vmem_limit_bytes=VMEM_LIMIT),
        name="in_proj",
    )(x, x, x, shift, scale, cos_t, sin_t, w_in, conv_w, conv_b, cln_g, cln_b)


def _mixer_body(sink_ref, act_ref,
                kT_ref, kT_prev_ref, kT_next_ref, v_ref, v_prev_ref, v_next_ref,
                kcT_ref, vc_ref, band_ref, w_out_ref,
                x_ref, gate_ref, pln_g_ref, pln_b_ref,
                o_ref,
                v4_ref, kcT4_ref, vc4_ref, mixed_ref, p_ref, esink_ref, kTw_ref, bias_ref):
    tq = x_ref.shape[0]
    i = pl.program_id(1)
    is_first = i == 0
    is_last = i == pl.num_programs(1) - 1
    n_qblk = tq // QBLK

    n_ctx_tiles = kcT_ref.shape[-1] // LANES
    n_loc_tiles = (QBLK + 2 * WINDOW) // LANES
    loc_cols = n_loc_tiles * LANES
    n_tiles = n_loc_tiles + n_ctx_tiles

    @pl.when(is_first)
    def _():
        for h in range(N_KV_HEADS):
            for g in range(GQA_GROUP):
                kcT4_ref[h, g * HEAD_DIM:(g + 1) * HEAD_DIM, :] = kcT_ref[h * HEAD_DIM:(h + 1) * HEAD_DIM, :]
            vc4_ref[h, :, 0:LANES] = vc_ref[:, h * LANES:(h + 1) * LANES]
            vc4_ref[h, :, LANES:] = jnp.ones((vc4_ref.shape[1], LANES), BF16)
            v4_ref[h, :, LANES:] = jnp.ones((v4_ref.shape[1], LANES), BF16)

    for h in range(N_KV_HEADS):
        lanes = slice(h * LANES, (h + 1) * LANES)
        v4_ref[h, 0:WINDOW, 0:LANES] = v_prev_ref[:, lanes]
        v4_ref[h, WINDOW:WINDOW + tq, 0:LANES] = v_ref[:, lanes]
        v4_ref[h, WINDOW + tq:, 0:LANES] = v_next_ref[:, lanes]

    def head_lanes(g):
        lane = lax.broadcasted_iota(jnp.int32, (1, GQA_GROUP * HEAD_DIM), 1)
        return (lane >= g * HEAD_DIM) & (lane < (g + 1) * HEAD_DIM)

    for j in range(n_qblk):
        for t in range(n_loc_tiles):
            c0 = j * QBLK - WINDOW + t * LANES
            for h in range(N_KV_HEADS):
                rows = slice(h * HEAD_DIM, (h + 1) * HEAD_DIM)
                if c0 < 0:
                    src = kT_prev_ref[rows, :]
                elif c0 >= tq:
                    src = kT_next_ref[rows, :]
                else:
                    src = kT_ref[rows, c0:c0 + LANES]
                for g in range(GQA_GROUP):
                    kTw_ref[h, j, g * HEAD_DIM:(g + 1) * HEAD_DIM, t * LANES:(t + 1) * LANES] = src
        lo = band_ref[:, 0:WINDOW]
        hi = band_ref[:, QBLK + WINDOW:]
        if j == 0:
            lo = lo + jnp.where(is_first, NEG_INF, 0.0)
        if j == n_qblk - 1:
            hi = hi + jnp.where(is_last, NEG_INF, 0.0)
        bias_ref[j, :, 0:WINDOW] = lo
        bias_ref[j, :, WINDOW:] = hi

    mixed_ref[:, ATTN_WIDTH:] = act_ref[:, ACT_C:ACT_C + CONV_WIDTH]

    for j in range(n_qblk):
        q0 = j * QBLK
        for h in range(N_KV_HEADS):
            hl = slice(h * GQA_GROUP * HEAD_DIM, (h + 1) * GQA_GROUP * HEAD_DIM)
            qb = act_ref[q0:q0 + QBLK, ACT_Q + hl.start:ACT_Q + hl.stop]
            lhs = jnp.concatenate(
                [jnp.where(head_lanes(g), qb, jnp.zeros_like(qb)) for g in range(GQA_GROUP)], axis=0)
            s_loc = jnp.dot(lhs, kTw_ref[h, j], preferred_element_type=F32)
            s_ctx = jnp.dot(lhs, kcT4_ref[h], preferred_element_type=F32)
            for rc in range(0, GQA_GROUP * QBLK, SM_ROWS):
                g, rr = divmod(rc, QBLK)
                sink = sink_ref[h * GQA_GROUP + g] * LOG2E
                tiles = [s_loc[rc:rc + SM_ROWS, t * LANES:(t + 1) * LANES] for t in range(n_loc_tiles)]
                tiles += [s_ctx[rc:rc + SM_ROWS, t * LANES:(t + 1) * LANES] for t in range(n_ctx_tiles)]
                tiles[0] = tiles[0] + bias_ref[j, rr:rr + SM_ROWS, 0:WINDOW]
                tiles[n_loc_tiles - 1] = tiles[n_loc_tiles - 1] + bias_ref[j, rr:rr + SM_ROWS, WINDOW:]
                m = jnp.max(functools.reduce(jnp.maximum, tiles), axis=-1, keepdims=True)
                m = jnp.maximum(m, sink)
                esink_ref[h, rc:rc + SM_ROWS, :] = jnp.exp2(sink - m)
                for t in range(n_tiles):
                    p_ref[h, rc:rc + SM_ROWS, t * LANES:(t + 1) * LANES] = jnp.exp2(tiles[t] - m).astype(BF16)
            r = (jnp.dot(p_ref[h, :, 0:loc_cols], v4_ref[h, q0:q0 + loc_cols, :], preferred_element_type=F32)
                 + jnp.dot(p_ref[h, :, loc_cols:], vc4_ref[h], preferred_element_type=F32))
            out = r[:, 0:LANES] * (1.0 / (r[:, LANES:] + esink_ref[h]))
            first_of_pair = lax.broadcasted_iota(jnp.int32, (1, LANES), 1) < HEAD_DIM
            attn = jnp.concatenate(
                [jnp.where(first_of_pair, out[(2 * k) * QBLK:(2 * k + 1) * QBLK],
                           out[(2 * k + 1) * QBLK:(2 * k + 2) * QBLK]) for k in range(GQA_GROUP // 2)], axis=1)
            sag = act_ref[q0:q0 + QBLK, ACT_SAG + hl.start:ACT_SAG + hl.stop].astype(F32)
            mixed_ref[q0:q0 + QBLK, hl] = (attn * sag).astype(BF16)

    y = jnp.dot(mixed_ref[...], w_out_ref[...], preferred_element_type=F32)
    z = x_ref[...] + (gate_ref[...] * (1.0 / ALPHA)) * y
    o_ref[...] = _norm_rows(z, LN_EPS / ALPHA ** 2) * pln_g_ref[...] + pln_b_ref[...]


def _mixer(sink, act, kT, vv, kcT, vc, band, w_out, x, gate, pln_g, pln_b):
    b, n, d = x.shape
    tq = TILE_ROWS
    m = kcT.shape[-1]
    qpt = tq // QBLK
    n_q = n // QBLK
    const2 = lambda bi, i: (0, 0)
    return pl.pallas_call(
        _mixer_body,
        grid=(b, n // tq),
        in_specs=[
            pl.BlockSpec(memory_space=pltpu.SMEM),
            pl.BlockSpec((None, tq, ACT_WIDTH), lambda bi, i: (bi, i, 0)),
            pl.BlockSpec((None, KV_WIDTH, tq), lambda bi, i: (bi, 0, i)),
            pl.BlockSpec((None, KV_WIDTH, WINDOW), lambda bi, i: (bi, 0, jnp.maximum(i * qpt - 1, 0))),
            pl.BlockSpec((None, KV_WIDTH, WINDOW), lambda bi, i: (bi, 0, jnp.minimum((i + 1) * qpt, n_q - 1))),
            pl.BlockSpec((None, tq, 2 * LANES), lambda bi, i: (bi, i, 0)),
            pl.BlockSpec((None, WINDOW, 2 * LANES), lambda bi, i: (bi, jnp.maximum(i * qpt - 1, 0), 0)),
            pl.BlockSpec((None, WINDOW, 2 * LANES), lambda bi, i: (bi, jnp.minimum((i + 1) * qpt, n_q - 1), 0)),
            pl.BlockSpec((None, KV_WIDTH, m), lambda bi, i: (bi, 0, 0)),
            pl.BlockSpec((None, m, 2 * LANES), lambda bi, i: (bi, 0, 0)),
            pl.BlockSpec((QBLK, QBLK + 2 * WINDOW), const2),
            pl.BlockSpec((d, d), const2),
            pl.BlockSpec((None, tq, d), lambda bi, i: (bi, i, 0)),
            pl.BlockSpec((None, 1, d), lambda bi, i: (bi, 0, 0)),
            pl.BlockSpec((1, d), const2),
            pl.BlockSpec((1, d), const2),
        ],
        out_specs=pl.BlockSpec((None, tq, d), lambda bi, i: (bi, i, 0)),
        out_shape=jax.ShapeDtypeStruct((b, n, d), F32),
        scratch_shapes=[
            pltpu.VMEM((N_KV_HEADS, tq + 2 * WINDOW, 2 * LANES), BF16),
            pltpu.VMEM((N_KV_HEADS, GQA_GROUP * HEAD_DIM, m), BF16),
            pltpu.VMEM((N_KV_HEADS, m, 2 * LANES), BF16),
            pltpu.VMEM((tq, d), BF16),
            pltpu.VMEM((N_KV_HEADS, GQA_GROUP * QBLK, QBLK + 2 * WINDOW + m), BF16),
            pltpu.VMEM((N_KV_HEADS, GQA_GROUP * QBLK, 1), F32),
            pltpu.VMEM((N_KV_HEADS, tq // QBLK, GQA_GROUP * HEAD_DIM, QBLK + 2 * WINDOW), BF16),
            pltpu.VMEM((tq // QBLK, QBLK, 2 * WINDOW), F32),
        ],
        compiler_params=pltpu.CompilerParams(
            dimension_semantics=("arbitrary", "arbitrary"), vmem_limit_bytes=VMEM_LIMIT),
        name="mixer",
    )(sink, act, kT, kT, kT, vv, vv, vv, kcT, vc, band, w_out, x, gate, pln_g, pln_b)


def _rope_tables(n):
    rows = n // GRID_W
    row = jnp.repeat(jnp.arange(rows, dtype=F32), GRID_W)
    colp = jnp.tile(jnp.arange(GRID_W, dtype=F32), rows)
    inv_freq = ROPE_BASE ** (-jnp.arange(0, ROT_AXIS_DIM, 2, dtype=F32) / ROT_AXIS_DIM)
    ang_row = row[:, None] * inv_freq
    ang_col = colp[:, None] * inv_freq
    cos_h = jnp.concatenate([jnp.cos(ang_row), jnp.cos(ang_row), jnp.cos(ang_col), jnp.cos(ang_col)], axis=-1)
    sin_h = jnp.concatenate([-jnp.sin(ang_row), jnp.sin(ang_row), -jnp.sin(ang_col), jnp.sin(ang_col)], axis=-1)
    reps = LANES // HEAD_DIM
    return jnp.tile(cos_h, (1, reps)), jnp.tile(sin_h, (1, reps))


def _band_bias():
    rel = jnp.arange(QBLK + 2 * WINDOW)[None, :] - jnp.arange(QBLK)[:, None]
    return jnp.where((rel >= 0) & (rel <= 2 * WINDOW), 0.0, NEG_INF).astype(F32)


def kernel(x, c, ctx, c_ctx, w_ada, b_ada, w_in, attn_sink, conv_w, conv_b, conv_ln_g, conv_ln_b,
           w_out, post_ln_g, post_ln_b):
    assert w_ada.shape[0] == DEPTH
    b, n, d = x.shape
    cond = jnp.zeros((ADA_ROWS, d), F32).at[:b].set(c).at[b].set(c_ctx)
    mods = _adaln(cond, w_ada[0], b_ada[0][None, :])
    shift, scale, gate = (mods[:b, k * d:(k + 1) * d][:, None, :] for k in range(3))
    shift_c, scale_c = (mods[b:b + 1, k * d:(k + 1) * d] for k in range(2))

    w_in_b = w_in[0].astype(BF16)
    kcT, vc = _ctx_kv(ctx, shift_c, scale_c, w_in_b[:, K_OFF:AG_OFF])
    cos_t, sin_t = _rope_tables(n)
    conv_w_rep = jnp.broadcast_to(conv_w[0][:, None, :], (CONV_SIZE, SUBLANES, CONV_WIDTH))
    act, kT, vv = _in_proj(x, shift, scale, cos_t, sin_t, w_in_b,
                           conv_w_rep, conv_b[0][None, :], conv_ln_g[0][None, :], conv_ln_b[0][None, :])
    return _mixer(attn_sink[0], act, kT, vv, kcT, vc, _band_bias(),
                  w_out[0].astype(BF16), x, gate, post_ln_g[0][None, :], post_ln_b[0][None, :])
```

```python
import functools
import math

import jax
import jax.numpy as jnp
from jax import lax
from jax.experimental import pallas as pl
from jax.experimental.pallas import tpu as pltpu

F32 = jnp.float32
BF16 = jnp.bfloat16

D_MODEL = 1024
GRID_W = 64
HEAD_DIM = 64
N_Q_HEADS = 8
N_KV_HEADS = 2
GQA_GROUP = N_Q_HEADS // N_KV_HEADS
ATTN_WIDTH = N_Q_HEADS * HEAD_DIM
KV_WIDTH = N_KV_HEADS * HEAD_DIM
CONV_WIDTH = D_MODEL - ATTN_WIDTH
CONV_SIZE = 31
CONV_HALF = CONV_SIZE // 2
WINDOW = 128
QBLK = 128
ROPE_BASE = 10000.0
ROT_AXIS_DIM = HEAD_DIM // 2
ROT_HALF = ROT_AXIS_DIM // 2
LN_EPS = 1e-6
NEG_INF = -1e30
DEPTH = 1
ALPHA = (2.0 * DEPTH) ** 0.25

Q_OFF = 0
K_OFF = Q_OFF + ATTN_WIDTH
V_OFF = K_OFF + KV_WIDTH
AG_OFF = V_OFF + KV_WIDTH
CA_OFF = AG_OFF + ATTN_WIDTH
CB_OFF = CA_OFF + CONV_WIDTH
CG_OFF = CB_OFF + CONV_WIDTH
IN_WIDTH = CG_OFF + CONV_WIDTH

ACT_Q = 0
ACT_C = ATTN_WIDTH
ACT_SAG = ACT_C + CONV_WIDTH
ACT_WIDTH = ACT_SAG + ATTN_WIDTH

LANES = 128
SUBLANES = 8
CONV_ROWS = 64
CONV_COLS = 256
SM_ROWS = 32
CONV_PER_PROJ = 4
LOG2E = math.log2(math.e)
U_HALO = 16
CONV_WIN_EXTRA = ((U_HALO + CONV_HALF) // SUBLANES) * SUBLANES
TILE_ROWS = 512
MIXER_ROWS = 1024
ADA_ROWS = 40
ADA_COLS = 512
VMEM_LIMIT = 56 * 1024 * 1024


def _norm_rows(x, eps=LN_EPS):
    mu = jnp.mean(x, axis=-1, keepdims=True)
    xc = x - mu
    var = jnp.mean(xc * xc, axis=-1, keepdims=True)
    return xc * lax.rsqrt(var + eps)


def _sigmoid(x):
    return 0.5 + 0.5 * jnp.tanh(0.5 * x)


def _silu(x):
    hx = 0.5 * x
    return hx + hx * jnp.tanh(hx)


def _adaln_body(c_ref, w_ref, b_ref, o_ref):
    o_ref[...] = jnp.dot(_silu(c_ref[...]), w_ref[...], preferred_element_type=F32) + b_ref[...]


def _adaln(cond, w_ada, b_ada):
    n_cols = w_ada.shape[1]
    return pl.pallas_call(
        _adaln_body,
        grid=(n_cols // ADA_COLS,),
        in_specs=[
            pl.BlockSpec((ADA_ROWS, D_MODEL), lambda j: (0, 0)),
            pl.BlockSpec((D_MODEL, ADA_COLS), lambda j: (0, j)),
            pl.BlockSpec((1, ADA_COLS), lambda j: (0, j)),
        ],
        out_specs=pl.BlockSpec((ADA_ROWS, ADA_COLS), lambda j: (0, j)),
        out_shape=jax.ShapeDtypeStruct((ADA_ROWS, n_cols), F32),
        compiler_params=pltpu.CompilerParams(dimension_semantics=("arbitrary",)),
        name="adaln",
    )(cond, w_ada, b_ada)


def _dup_heads(v):
    lane = lax.broadcasted_iota(jnp.int32, (1, LANES), 1)
    swapped = pltpu.roll(v, HEAD_DIM, 1)
    low = lane < HEAD_DIM
    return jnp.where(low, v, swapped), jnp.where(low, swapped, v)


def _ctx_kv_body(ctx_ref, shift_ref, scale_ref, w_ref, kT_ref, v_ref):
    h = _norm_rows(ctx_ref[...]) * (1.0 + scale_ref[...]) + shift_ref[...]
    kv = jnp.dot(h.astype(BF16), w_ref[...], preferred_element_type=F32)
    kT_ref[...] = kv[:, :KV_WIDTH].T.astype(BF16)
    v0, v1 = _dup_heads(kv[:, KV_WIDTH:])
    v_ref[:, :LANES] = v0.astype(BF16)
    v_ref[:, LANES:] = v1.astype(BF16)


def _ctx_kv(ctx, shift_c, scale_c, w_kv):
    b, m, d = ctx.shape
    return pl.pallas_call(
        _ctx_kv_body,
        grid=(b,),
        in_specs=[
            pl.BlockSpec((None, m, d), lambda i: (i, 0, 0)),
            pl.BlockSpec((1, d), lambda i: (0, 0)),
            pl.BlockSpec((1, d), lambda i: (0, 0)),
            pl.BlockSpec((d, 2 * KV_WIDTH), lambda i: (0, 0)),
        ],
        out_specs=[
            pl.BlockSpec((None, KV_WIDTH, m), lambda i: (i, 0, 0)),
            pl.BlockSpec((None, m, 2 * LANES), lambda i: (i, 0, 0)),
        ],
        out_shape=[
            jax.ShapeDtypeStruct((b, KV_WIDTH, m), BF16),
            jax.ShapeDtypeStruct((b, m, 2 * LANES), BF16),
        ],
        compiler_params=pltpu.CompilerParams(dimension_semantics=("arbitrary",)),
        name="ctx_kv",
    )(ctx, shift_c, scale_c, w_kv)


def _in_proj_body(x_ref, x_prev_ref, x_next_ref, shift_ref, scale_ref, cos_ref, sin_ref, w_ref,
                  conv_w_ref, conv_b_ref, cln_g_ref, cln_b_ref,
                  act_ref, kT_ref, v_ref,
                  hbuf_ref, ubuf_ref, ushift_ref, scg_ref):
    tm = x_ref.shape[0]
    i = pl.program_id(1)
    is_first = i == 0
    is_last = i == pl.num_programs(1) - 1

    def modulate(xv):
        return (_norm_rows(xv) * (1.0 + scale_ref[...]) + shift_ref[...]).astype(BF16)

    hbuf_ref[0:U_HALO, :] = modulate(x_prev_ref[...])
    hbuf_ref[U_HALO:U_HALO + tm, :] = modulate(x_ref[...])
    hbuf_ref[U_HALO + tm:, :] = modulate(x_next_ref[...])
    main = slice(U_HALO, U_HALO + tm)

    cos = cos_ref[...]
    sin = sin_ref[...]
    lane = lax.broadcasted_iota(jnp.int32, (1, LANES), 1)
    first_half = (lane % ROT_AXIS_DIM) < ROT_HALF

    def rope(t):
        partner = jnp.where(first_half, pltpu.roll(t, LANES - ROT_HALF, 1), pltpu.roll(t, ROT_HALF, 1))
        return t * cos + partner * sin

    def proj(rows, lo, width):
        return jnp.dot(hbuf_ref[rows, :], w_ref[:, lo:lo + width], preferred_element_type=F32)

    glu = proj(slice(None), CA_OFF, 2 * CONV_WIDTH)
    u = glu[:, :CONV_WIDTH] * _sigmoid(glu[:, CONV_WIDTH:])
    ubuf_ref[0:U_HALO, :] = jnp.where(is_first, 0.0, u[0:U_HALO])
    ubuf_ref[U_HALO:U_HALO + tm, :] = u[U_HALO:U_HALO + tm]
    ubuf_ref[U_HALO + tm:, :] = jnp.where(is_last, 0.0, u[U_HALO + tm:])
    shift_rows = ushift_ref.shape[1]
    for r in range(1, SUBLANES):
        ushift_ref[r - 1] = ubuf_ref[r:r + shift_rows, :]
    scg_ref[...] = _silu(proj(main, CG_OFF, CONV_WIDTH))

    def projection_work():
        for c2 in range(ATTN_WIDTH // (2 * LANES)):
            q = proj(main, Q_OFF + c2 * 2 * LANES, 2 * LANES)
            for c in range(2):
                dst = ACT_Q + (2 * c2 + c) * LANES
                act_ref[:, dst:dst + LANES] = (
                    rope(q[:, c * LANES:(c + 1) * LANES]) * (HEAD_DIM ** -0.5 * LOG2E)).astype(BF16)
            yield
        kv = proj(main, K_OFF, 2 * KV_WIDTH)
        kT_ref[...] = rope(kv[:, :KV_WIDTH]).T.astype(BF16)
        v0, v1 = _dup_heads(kv[:, KV_WIDTH:])
        v_ref[:, :LANES] = v0.astype(BF16)
        v_ref[:, LANES:] = v1.astype(BF16)
        yield
        for c2 in range(ATTN_WIDTH // (2 * LANES)):
            lo = c2 * 2 * LANES
            act_ref[:, ACT_SAG + lo:ACT_SAG + lo + 2 * LANES] = _silu(proj(main, AG_OFF + lo, 2 * LANES)).astype(BF16)
            yield

    def conv_work():
        groups = CONV_ROWS // SUBLANES
        for r0 in range(0, tm, CONV_ROWS):
            slabs = []
            for cb in range(CONV_WIDTH // CONV_COLS):
                ch = slice(cb * CONV_COLS, (cb + 1) * CONV_COLS)
                acc = jnp.zeros((groups, SUBLANES, CONV_COLS), F32) + conv_b_ref[:, ch]
                for r in range(SUBLANES):
                    taps = [t for t in range(CONV_SIZE) if (U_HALO - CONV_HALF + t) % SUBLANES == r]
                    src = ubuf_ref if r == 0 else ushift_ref.at[r - 1]
                    win = src[r0:r0 + CONV_ROWS + CONV_WIN_EXTRA, ch].reshape(-1, SUBLANES, CONV_COLS)
                    for t in taps:
                        a = (U_HALO - CONV_HALF + t) // SUBLANES
                        acc = acc + win[a:a + groups] * conv_w_ref[t, :, ch][None]
                slabs.append(acc.reshape(CONV_ROWS, CONV_COLS))
                yield
            cn = _silu(_norm_rows(jnp.concatenate(slabs, axis=1)) * cln_g_ref[...] + cln_b_ref[...])
            act_ref[r0:r0 + CONV_ROWS, ACT_C:ACT_C + CONV_WIDTH] = (
                cn * scg_ref[r0:r0 + CONV_ROWS, :]).astype(BF16)
            yield

    conv, projections = conv_work(), projection_work()
    conv_left = proj_left = True
    while conv_left or proj_left:
        for _ in range(CONV_PER_PROJ):
            conv_left = conv_left and next(conv, "done") != "done"
        proj_left = proj_left and next(projections, "done") != "done"


def _in_proj(x, shift, scale, cos_t, sin_t, w_in, conv_w, conv_b, cln_g, cln_b):
    b, n, d = x.shape
    tm = TILE_ROWS
    hpt = tm // U_HALO
    n_h = n // U_HALO
    const2 = lambda bi, i: (0, 0)
    return pl.pallas_call(
        _in_proj_body,
        grid=(b, n // tm),
        in_specs=[
            pl.BlockSpec((None, tm, d), lambda bi, i: (bi, i, 0)),
            pl.BlockSpec((None, U_HALO, d), lambda bi, i: (bi, jnp.maximum(i * hpt - 1, 0), 0)),
            pl.BlockSpec((None, U_HALO, d), lambda bi, i: (bi, jnp.minimum((i + 1) * hpt, n_h - 1), 0)),
            pl.BlockSpec((None, 1, d), lambda bi, i: (bi, 0, 0)),
            pl.BlockSpec((None, 1, d), lambda bi, i: (bi, 0, 0)),
            pl.BlockSpec((tm, LANES), lambda bi, i: (i, 0)),
            pl.BlockSpec((tm, LANES), lambda bi, i: (i, 0)),
            pl.BlockSpec((d, IN_WIDTH), const2),
            pl.BlockSpec((CONV_SIZE, SUBLANES, CONV_WIDTH), lambda bi, i: (0, 0, 0)),
            pl.BlockSpec((1, CONV_WIDTH), const2),
            pl.BlockSpec((1, CONV_WIDTH), const2),
            pl.BlockSpec((1, CONV_WIDTH), const2),
        ],
        out_specs=[
            pl.BlockSpec((None, tm, ACT_WIDTH), lambda bi, i: (bi, i, 0)),
            pl.BlockSpec((None, KV_WIDTH, tm), lambda bi, i: (bi, 0, i)),
            pl.BlockSpec((None, tm, 2 * LANES), lambda bi, i: (bi, i, 0)),
        ],
        out_shape=[
            jax.ShapeDtypeStruct((b, n, ACT_WIDTH), BF16),
            jax.ShapeDtypeStruct((b, KV_WIDTH, n), BF16),
            jax.ShapeDtypeStruct((b, n, 2 * LANES), BF16),
        ],
        scratch_shapes=[
            pltpu.VMEM((tm + 2 * U_HALO, d), BF16),
            pltpu.VMEM((tm + 2 * U_HALO, CONV_WIDTH), F32),
            pltpu.VMEM((SUBLANES - 1, tm + 2 * U_HALO - SUBLANES, CONV_WIDTH), F32),
            pltpu.VMEM((tm, CONV_WIDTH), F32),
        ],
        compiler_params=pltpu.CompilerParams(
            dimension_semantics=("arbitrary", "arbitrary"), vmem_limit_bytes=VMEM_LIMIT),
        name="in_proj",
    )(x, x, x, shift, scale, cos_t, sin_t, w_in, conv_w, conv_b, cln_g, cln_b)


def _mixer_body(sink_ref, act_ref,
                kT_ref, kT_prev_ref, kT_next_ref, v_ref, v_prev_ref, v_next_ref,
                kcT_ref, vc_ref, band_ref, w_out_ref,
                x_ref, gate_ref, pln_g_ref, pln_b_ref,
                o_ref,
                v4_ref, kcT4_ref, vc4_ref, mixed_ref, p_ref, esink_ref, kTw_ref, bias_ref):
    tq = x_ref.shape[0]
    i = pl.program_id(1)
    is_first = i == 0
    is_last = i == pl.num_programs(1) - 1
    n_qblk = tq // QBLK

    n_ctx_tiles = kcT_ref.shape[-1] // LANES
    n_loc_tiles = (QBLK + 2 * WINDOW) // LANES
    loc_cols = n_loc_tiles * LANES
    n_tiles = n_loc_tiles + n_ctx_tiles

    @pl.when(is_first)
    def _():
        for h in range(N_KV_HEADS):
            for g in range(GQA_GROUP):
                kcT4_ref[h, g * HEAD_DIM:(g + 1) * HEAD_DIM, :] = kcT_ref[h * HEAD_DIM:(h + 1) * HEAD_DIM, :]
            vc4_ref[h, :, 0:LANES] = vc_ref[:, h * LANES:(h + 1) * LANES]
            vc4_ref[h, :, LANES:] = jnp.ones((vc4_ref.shape[1], LANES), BF16)
            v4_ref[h, :, LANES:] = jnp.ones((v4_ref.shape[1], LANES), BF16)

    for h in range(N_KV_HEADS):
        lanes = slice(h * LANES, (h + 1) * LANES)
        v4_ref[h, 0:WINDOW, 0:LANES] = v_prev_ref[:, lanes]
        v4_ref[h, WINDOW:WINDOW + tq, 0:LANES] = v_ref[:, lanes]
        v4_ref[h, WINDOW + tq:, 0:LANES] = v_next_ref[:, lanes]

    def head_lanes(g):
        lane = lax.broadcasted_iota(jnp.int32, (1, GQA_GROUP * HEAD_DIM), 1)
        return (lane >= g * HEAD_DIM) & (lane < (g + 1) * HEAD_DIM)

    for j in range(n_qblk):
        for t in range(n_loc_tiles):
            c0 = j * QBLK - WINDOW + t * LANES
            for h in range(N_KV_HEADS):
                rows = slice(h * HEAD_DIM, (h + 1) * HEAD_DIM)
                if c0 < 0:
                    src = kT_prev_ref[rows, :]
                elif c0 >= tq:
                    src = kT_next_ref[rows, :]
                else:
                    src = kT_ref[rows, c0:c0 + LANES]
                for g in range(GQA_GROUP):
                    kTw_ref[h, j, g * HEAD_DIM:(g + 1) * HEAD_DIM, t * LANES:(t + 1) * LANES] = src
        lo = band_ref[:, 0:WINDOW]
        hi = band_ref[:, QBLK + WINDOW:]
        if j == 0:
            lo = lo + jnp.where(is_first, NEG_INF, 0.0)
        if j == n_qblk - 1:
            hi = hi + jnp.where(is_last, NEG_INF, 0.0)
        bias_ref[j, :, 0:WINDOW] = lo
        bias_ref[j, :, WINDOW:] = hi

    mixed_ref[:, ATTN_WIDTH:] = act_ref[:, ACT_C:ACT_C + CONV_WIDTH]

    for j in range(n_qblk):
        q0 = j * QBLK
        for h in range(N_KV_HEADS):
            hl = slice(h * GQA_GROUP * HEAD_DIM, (h + 1) * GQA_GROUP * HEAD_DIM)
            qb = act_ref[q0:q0 + QBLK, ACT_Q + hl.start:ACT_Q + hl.stop]
            lhs = jnp.concatenate(
                [jnp.where(head_lanes(g), qb, jnp.zeros_like(qb)) for g in range(GQA_GROUP)], axis=0)
            s_loc = jnp.dot(lhs, kTw_ref[h, j], preferred_element_type=F32)
            s_ctx = jnp.dot(lhs, kcT4_ref[h], preferred_element_type=F32)
            for rc in range(0, GQA_GROUP * QBLK, SM_ROWS):
                g, rr = divmod(rc, QBLK)
                sink = sink_ref[h * GQA_GROUP + g] * LOG2E
                tiles = [s_loc[rc:rc + SM_ROWS, t * LANES:(t + 1) * LANES] for t in range(n_loc_tiles)]
                tiles += [s_ctx[rc:rc + SM_ROWS, t * LANES:(t + 1) * LANES] for t in range(n_ctx_tiles)]
                tiles[0] = tiles[0] + bias_ref[j, rr:rr + SM_ROWS, 0:WINDOW]
                tiles[n_loc_tiles - 1] = tiles[n_loc_tiles - 1] + bias_ref[j, rr:rr + SM_ROWS, WINDOW:]
                m = jnp.max(functools.reduce(jnp.maximum, tiles), axis=-1, keepdims=True)
                m = jnp.maximum(m, sink)
                esink_ref[h, rc:rc + SM_ROWS, :] = jnp.exp2(sink - m)
                for t in range(n_tiles):
                    p_ref[h, rc:rc + SM_ROWS, t * LANES:(t + 1) * LANES] = jnp.exp2(tiles[t] - m).astype(BF16)
            r = (jnp.dot(p_ref[h, :, 0:loc_cols], v4_ref[h, q0:q0 + loc_cols, :], preferred_element_type=F32)
                 + jnp.dot(p_ref[h, :, loc_cols:], vc4_ref[h], preferred_element_type=F32))
            out = r[:, 0:LANES] * (1.0 / (r[:, LANES:] + esink_ref[h]))
            first_of_pair = lax.broadcasted_iota(jnp.int32, (1, LANES), 1) < HEAD_DIM
            attn = jnp.concatenate(
                [jnp.where(first_of_pair, out[(2 * k) * QBLK:(2 * k + 1) * QBLK],
                           out[(2 * k + 1) * QBLK:(2 * k + 2) * QBLK]) for k in range(GQA_GROUP // 2)], axis=1)
            sag = act_ref[q0:q0 + QBLK, ACT_SAG + hl.start:ACT_SAG + hl.stop].astype(F32)
            mixed_ref[q0:q0 + QBLK, hl] = (attn * sag).astype(BF16)

    y = jnp.dot(mixed_ref[...], w_out_ref[...], preferred_element_type=F32)
    z = x_ref[...] + (gate_ref[...] * (1.0 / ALPHA)) * y
    o_ref[...] = _norm_rows(z, LN_EPS / ALPHA ** 2) * pln_g_ref[...] + pln_b_ref[...]


def _mixer(sink, act, kT, vv, kcT, vc, band, w_out, x, gate, pln_g, pln_b):
    b, n, d = x.shape
    tq = MIXER_ROWS
    m = kcT.shape[-1]
    qpt = tq // QBLK
    n_q = n // QBLK
    const2 = lambda bi, i: (0, 0)
    return pl.pallas_call(
        _mixer_body,
        grid=(b, n // tq),
        in_specs=[
            pl.BlockSpec(memory_space=pltpu.SMEM),
            pl.BlockSpec((None, tq, ACT_WIDTH), lambda bi, i: (bi, i, 0)),
            pl.BlockSpec((None, KV_WIDTH, tq), lambda bi, i: (bi, 0, i)),
            pl.BlockSpec((None, KV_WIDTH, WINDOW), lambda bi, i: (bi, 0, jnp.maximum(i * qpt - 1, 0))),
            pl.BlockSpec((None, KV_WIDTH, WINDOW), lambda bi, i: (bi, 0, jnp.minimum((i + 1) * qpt, n_q - 1))),
            pl.BlockSpec((None, tq, 2 * LANES), lambda bi, i: (bi, i, 0)),
            pl.BlockSpec((None, WINDOW, 2 * LANES), lambda bi, i: (bi, jnp.maximum(i * qpt - 1, 0), 0)),
            pl.BlockSpec((None, WINDOW, 2 * LANES), lambda bi, i: (bi, jnp.minimum((i + 1) * qpt, n_q - 1), 0)),
            pl.BlockSpec((None, KV_WIDTH, m), lambda bi, i: (bi, 0, 0)),
            pl.BlockSpec((None, m, 2 * LANES), lambda bi, i: (bi, 0, 0)),
            pl.BlockSpec((QBLK, QBLK + 2 * WINDOW), const2),
            pl.BlockSpec((d, d), const2),
            pl.BlockSpec((None, tq, d), lambda bi, i: (bi, i, 0)),
            pl.BlockSpec((None, 1, d), lambda bi, i: (bi, 0, 0)),
            pl.BlockSpec((1, d), const2),
            pl.BlockSpec((1, d), const2),
        ],
        out_specs=pl.BlockSpec((None, tq, d), lambda bi, i: (bi, i, 0)),
        out_shape=jax.ShapeDtypeStruct((b, n, d), F32),
        scratch_shapes=[
            pltpu.VMEM((N_KV_HEADS, tq + 2 * WINDOW, 2 * LANES), BF16),
            pltpu.VMEM((N_KV_HEADS, GQA_GROUP * HEAD_DIM, m), BF16),
            pltpu.VMEM((N_KV_HEADS, m, 2 * LANES), BF16),
            pltpu.VMEM((tq, d), BF16),
            pltpu.VMEM((N_KV_HEADS, GQA_GROUP * QBLK, QBLK + 2 * WINDOW + m), BF16),
            pltpu.VMEM((N_KV_HEADS, GQA_GROUP * QBLK, 1), F32),
            pltpu.VMEM((N_KV_HEADS, tq // QBLK, GQA_GROUP * HEAD_DIM, QBLK + 2 * WINDOW), BF16),
            pltpu.VMEM((tq // QBLK, QBLK, 2 * WINDOW), F32),
        ],
        compiler_params=pltpu.CompilerParams(
            dimension_semantics=("arbitrary", "arbitrary"), vmem_limit_bytes=VMEM_LIMIT),
        name="mixer",
    )(sink, act, kT, kT, kT, vv, vv, vv, kcT, vc, band, w_out, x, gate, pln_g, pln_b)


def _rope_tables(n):
    rows = n // GRID_W
    row = jnp.repeat(jnp.arange(rows, dtype=F32), GRID_W)
    colp = jnp.tile(jnp.arange(GRID_W, dtype=F32), rows)
    inv_freq = ROPE_BASE ** (-jnp.arange(0, ROT_AXIS_DIM, 2, dtype=F32) / ROT_AXIS_DIM)
    ang_row = row[:, None] * inv_freq
    ang_col = colp[:, None] * inv_freq
    cos_h = jnp.concatenate([jnp.cos(ang_row), jnp.cos(ang_row), jnp.cos(ang_col), jnp.cos(ang_col)], axis=-1)
    sin_h = jnp.concatenate([-jnp.sin(ang_row), jnp.sin(ang_row), -jnp.sin(ang_col), jnp.sin(ang_col)], axis=-1)
    reps = LANES // HEAD_DIM
    return jnp.tile(cos_h, (1, reps)), jnp.tile(sin_h, (1, reps))


def _band_bias():
    rel = jnp.arange(QBLK + 2 * WINDOW)[None, :] - jnp.arange(QBLK)[:, None]
    return jnp.where((rel >= 0) & (rel <= 2 * WINDOW), 0.0, NEG_INF).astype(F32)


def kernel(x, c, ctx, c_ctx, w_ada, b_ada, w_in, attn_sink, conv_w, conv_b, conv_ln_g, conv_ln_b,
           w_out, post_ln_g, post_ln_b):
    assert w_ada.shape[0] == DEPTH
    b, n, d = x.shape
    cond = jnp.zeros((ADA_ROWS, d), F32).at[:b].set(c).at[b].set(c_ctx)
    mods = _adaln(cond, w_ada[0], b_ada[0][None, :])
    shift, scale, gate = (mods[:b, k * d:(k + 1) * d][:, None, :] for k in range(3))
    shift_c, scale_c = (mods[b:b + 1, k * d:(k + 1) * d] for k in range(2))

    w_in_b = w_in[0].astype(BF16)
    kcT, vc = _ctx_kv(ctx, shift_c, scale_c, w_in_b[:, K_OFF:AG_OFF])
    cos_t, sin_t = _rope_tables(n)
    conv_w_rep = jnp.broadcast_to(conv_w[0][:, None, :], (CONV_SIZE, SUBLANES, CONV_WIDTH))
    act, kT, vv = _in_proj(x, shift, scale, cos_t, sin_t, w_in_b,
                           conv_w_rep, conv_b[0][None, :], conv_ln_g[0][None, :], conv_ln_b[0][None, :])
    return _mixer(attn_sink[0], act, kT, vv, kcT, vc, _band_bias(),
                  w_out[0].astype(BF16), x, gate, post_ln_g[0][None, :], post_ln_b[0][None, :])
```

```python
import functools
import math

import jax
import jax.numpy as jnp
from jax import lax
from jax.experimental import pallas as pl
from jax.experimental.pallas import tpu as pltpu

F32 = jnp.float32
BF16 = jnp.bfloat16

D_MODEL = 1024
GRID_W = 64
HEAD_DIM = 64
N_Q_HEADS = 8
N_KV_HEADS = 2
GQA_GROUP = N_Q_HEADS // N_KV_HEADS
ATTN_WIDTH = N_Q_HEADS * HEAD_DIM
KV_WIDTH = N_KV_HEADS * HEAD_DIM
CONV_WIDTH = D_MODEL - ATTN_WIDTH
CONV_SIZE = 31
CONV_HALF = CONV_SIZE // 2
WINDOW = 128
QBLK = 128
ROPE_BASE = 10000.0
ROT_AXIS_DIM = HEAD_DIM // 2
ROT_HALF = ROT_AXIS_DIM // 2
LN_EPS = 1e-6
NEG_INF = -1e30
DEPTH = 1
ALPHA = (2.0 * DEPTH) ** 0.25

Q_OFF = 0
K_OFF = Q_OFF + ATTN_WIDTH
V_OFF = K_OFF + KV_WIDTH
AG_OFF = V_OFF + KV_WIDTH
CA_OFF = AG_OFF + ATTN_WIDTH
CB_OFF = CA_OFF + CONV_WIDTH
CG_OFF = CB_OFF + CONV_WIDTH
IN_WIDTH = CG_OFF + CONV_WIDTH

ACT_Q = 0
ACT_C = ATTN_WIDTH
ACT_SAG = ACT_C + CONV_WIDTH
ACT_WIDTH = ACT_SAG + ATTN_WIDTH

LANES = 128
SUBLANES = 8
CONV_ROWS = 64
CONV_COLS = 256
SM_ROWS = 32
CONV_PER_PROJ = 4
LOG2E = math.log2(math.e)
U_HALO = 16
TILE_ROWS = 512
MIXER_ROWS = 1024
ADA_ROWS = 40
ADA_COLS = 512
VMEM_LIMIT = 56 * 1024 * 1024


def _norm_rows(x, eps=LN_EPS):
    mu = jnp.mean(x, axis=-1, keepdims=True)
    xc = x - mu
    var = jnp.mean(xc * xc, axis=-1, keepdims=True)
    return xc * lax.rsqrt(var + eps)


def _sigmoid(x):
    return 0.5 + 0.5 * jnp.tanh(0.5 * x)


def _silu(x):
    hx = 0.5 * x
    return hx + hx * jnp.tanh(hx)


def _adaln_body(c_ref, w_ref, b_ref, o_ref):
    o_ref[...] = jnp.dot(_silu(c_ref[...]), w_ref[...], preferred_element_type=F32) + b_ref[...]


def _adaln(cond, w_ada, b_ada):
    n_cols = w_ada.shape[1]
    return pl.pallas_call(
        _adaln_body,
        grid=(n_cols // ADA_COLS,),
        in_specs=[
            pl.BlockSpec((ADA_ROWS, D_MODEL), lambda j: (0, 0)),
            pl.BlockSpec((D_MODEL, ADA_COLS), lambda j: (0, j)),
            pl.BlockSpec((1, ADA_COLS), lambda j: (0, j)),
        ],
        out_specs=pl.BlockSpec((ADA_ROWS, ADA_COLS), lambda j: (0, j)),
        out_shape=jax.ShapeDtypeStruct((ADA_ROWS, n_cols), F32),
        compiler_params=pltpu.CompilerParams(dimension_semantics=("arbitrary",)),
        name="adaln",
    )(cond, w_ada, b_ada)


def _dup_heads(v):
    lane = lax.broadcasted_iota(jnp.int32, (1, LANES), 1)
    swapped = pltpu.roll(v, HEAD_DIM, 1)
    low = lane < HEAD_DIM
    return jnp.where(low, v, swapped), jnp.where(low, swapped, v)


def _ctx_kv_body(ctx_ref, shift_ref, scale_ref, w_ref, kT_ref, v_ref):
    h = _norm_rows(ctx_ref[...]) * (1.0 + scale_ref[...]) + shift_ref[...]
    kv = jnp.dot(h.astype(BF16), w_ref[...], preferred_element_type=F32)
    kT_ref[...] = kv[:, :KV_WIDTH].T.astype(BF16)
    v0, v1 = _dup_heads(kv[:, KV_WIDTH:])
    v_ref[:, :LANES] = v0.astype(BF16)
    v_ref[:, LANES:] = v1.astype(BF16)


def _ctx_kv(ctx, shift_c, scale_c, w_kv):
    b, m, d = ctx.shape
    return pl.pallas_call(
        _ctx_kv_body,
        grid=(b,),
        in_specs=[
            pl.BlockSpec((None, m, d), lambda i: (i, 0, 0)),
            pl.BlockSpec((1, d), lambda i: (0, 0)),
            pl.BlockSpec((1, d), lambda i: (0, 0)),
            pl.BlockSpec((d, 2 * KV_WIDTH), lambda i: (0, 0)),
        ],
        out_specs=[
            pl.BlockSpec((None, KV_WIDTH, m), lambda i: (i, 0, 0)),
            pl.BlockSpec((None, m, 2 * LANES), lambda i: (i, 0, 0)),
        ],
        out_shape=[
            jax.ShapeDtypeStruct((b, KV_WIDTH, m), BF16),
            jax.ShapeDtypeStruct((b, m, 2 * LANES), BF16),
        ],
        compiler_params=pltpu.CompilerParams(dimension_semantics=("arbitrary",)),
        name="ctx_kv",
    )(ctx, shift_c, scale_c, w_kv)


def _in_proj_body(x_ref, x_prev_ref, x_next_ref, shift_ref, scale_ref, cos_ref, sin_ref, w_ref,
                  conv_w_ref, conv_b_ref, cln_g_ref, cln_b_ref,
                  act_ref, kT_ref, v_ref,
                  hbuf_ref, ubuf_ref, scg_ref):
    tm = x_ref.shape[0]
    i = pl.program_id(1)
    is_first = i == 0
    is_last = i == pl.num_programs(1) - 1

    def modulate(xv):
        return (_norm_rows(xv) * (1.0 + scale_ref[...]) + shift_ref[...]).astype(BF16)

    hbuf_ref[0:U_HALO, :] = modulate(x_prev_ref[...])
    hbuf_ref[U_HALO:U_HALO + tm, :] = modulate(x_ref[...])
    hbuf_ref[U_HALO + tm:, :] = modulate(x_next_ref[...])
    main = slice(U_HALO, U_HALO + tm)

    cos = cos_ref[...]
    sin = sin_ref[...]
    lane = lax.broadcasted_iota(jnp.int32, (1, LANES), 1)
    first_half = (lane % ROT_AXIS_DIM) < ROT_HALF

    def rope(t):
        partner = jnp.where(first_half, pltpu.roll(t, LANES - ROT_HALF, 1), pltpu.roll(t, ROT_HALF, 1))
        return t * cos + partner * sin

    def proj(rows, lo, width):
        return jnp.dot(hbuf_ref[rows, :], w_ref[:, lo:lo + width], preferred_element_type=F32)

    glu = proj(slice(None), CA_OFF, 2 * CONV_WIDTH)
    u = glu[:, :CONV_WIDTH] * _sigmoid(glu[:, CONV_WIDTH:])
    for c in range(CONV_WIDTH // LANES):
        cl = slice(c * LANES, (c + 1) * LANES)
        ubuf_ref[c, 0:U_HALO, :] = jnp.where(is_first, 0.0, u[0:U_HALO, cl])
        ubuf_ref[c, U_HALO:U_HALO + tm, :] = u[U_HALO:U_HALO + tm, cl]
        ubuf_ref[c, U_HALO + tm:, :] = jnp.where(is_last, 0.0, u[U_HALO + tm:, cl])
    scg_ref[...] = _silu(proj(main, CG_OFF, CONV_WIDTH))

    def projection_work():
        for c2 in range(ATTN_WIDTH // (2 * LANES)):
            q = proj(main, Q_OFF + c2 * 2 * LANES, 2 * LANES)
            for c in range(2):
                dst = ACT_Q + (2 * c2 + c) * LANES
                act_ref[:, dst:dst + LANES] = (
                    rope(q[:, c * LANES:(c + 1) * LANES]) * (HEAD_DIM ** -0.5 * LOG2E)).astype(BF16)
            yield
        kv = proj(main, K_OFF, 2 * KV_WIDTH)
        kT_ref[...] = rope(kv[:, :KV_WIDTH]).T.astype(BF16)
        v0, v1 = _dup_heads(kv[:, KV_WIDTH:])
        v_ref[:, :LANES] = v0.astype(BF16)
        v_ref[:, LANES:] = v1.astype(BF16)
        yield
        for c2 in range(ATTN_WIDTH // (2 * LANES)):
            lo = c2 * 2 * LANES
            act_ref[:, ACT_SAG + lo:ACT_SAG + lo + 2 * LANES] = _silu(proj(main, AG_OFF + lo, 2 * LANES)).astype(BF16)
            yield

    def conv_work():
        groups = CONV_ROWS // SUBLANES
        for r0 in range(0, tm, CONV_ROWS):
            slabs = []
            for cb in range(CONV_WIDTH // CONV_COLS):
                ch = slice(cb * CONV_COLS, (cb + 1) * CONV_COLS)
                acc = jnp.zeros((groups, SUBLANES, CONV_COLS), F32) + conv_b_ref[:, ch]
                for t in range(CONV_SIZE):
                    top = r0 + U_HALO - CONV_HALF + t
                    rows = jnp.concatenate(
                        [ubuf_ref[c, top:top + CONV_ROWS, :]
                         for c in range(cb * CONV_COLS // LANES, (cb + 1) * CONV_COLS // LANES)], axis=1)
                    acc = acc + rows.reshape(groups, SUBLANES, CONV_COLS) * conv_w_ref[t, :, ch][None]
                slabs.append(acc.reshape(CONV_ROWS, CONV_COLS))
                yield
            cn = _silu(_norm_rows(jnp.concatenate(slabs, axis=1)) * cln_g_ref[...] + cln_b_ref[...])
            act_ref[r0:r0 + CONV_ROWS, ACT_C:ACT_C + CONV_WIDTH] = (
                cn * scg_ref[r0:r0 + CONV_ROWS, :]).astype(BF16)
            yield

    conv, projections = conv_work(), projection_work()
    conv_left = proj_left = True
    while conv_left or proj_left:
        for _ in range(CONV_PER_PROJ):
            conv_left = conv_left and next(conv, "done") != "done"
        proj_left = proj_left and next(projections, "done") != "done"


def _in_proj(x, shift, scale, cos_t, sin_t, w_in, conv_w, conv_b, cln_g, cln_b):
    b, n, d = x.shape
    tm = TILE_ROWS
    hpt = tm // U_HALO
    n_h = n // U_HALO
    const2 = lambda bi, i: (0, 0)
    return pl.pallas_call(
        _in_proj_body,
        grid=(b, n // tm),
        in_specs=[
            pl.BlockSpec((None, tm, d), lambda bi, i: (bi, i, 0)),
            pl.BlockSpec((None, U_HALO, d), lambda bi, i: (bi, jnp.maximum(i * hpt - 1, 0), 0)),
            pl.BlockSpec((None, U_HALO, d), lambda bi, i: (bi, jnp.minimum((i + 1) * hpt, n_h - 1), 0)),
            pl.BlockSpec((None, 1, d), lambda bi, i: (bi, 0, 0)),
            pl.BlockSpec((None, 1, d), lambda bi, i: (bi, 0, 0)),
            pl.BlockSpec((tm, LANES), lambda bi, i: (i, 0)),
            pl.BlockSpec((tm, LANES), lambda bi, i: (i, 0)),
            pl.BlockSpec((d, IN_WIDTH), const2),
            pl.BlockSpec((CONV_SIZE, SUBLANES, CONV_WIDTH), lambda bi, i: (0, 0, 0)),
            pl.BlockSpec((1, CONV_WIDTH), const2),
            pl.BlockSpec((1, CONV_WIDTH), const2),
            pl.BlockSpec((1, CONV_WIDTH), const2),
        ],
        out_specs=[
            pl.BlockSpec((None, tm, ACT_WIDTH), lambda bi, i: (bi, i, 0)),
            pl.BlockSpec((None, KV_WIDTH, tm), lambda bi, i: (bi, 0, i)),
            pl.BlockSpec((None, tm, 2 * LANES), lambda bi, i: (bi, i, 0)),
        ],
        out_shape=[
            jax.ShapeDtypeStruct((b, n, ACT_WIDTH), BF16),
            jax.ShapeDtypeStruct((b, KV_WIDTH, n), BF16),
            jax.ShapeDtypeStruct((b, n, 2 * LANES), BF16),
        ],
        scratch_shapes=[
            pltpu.VMEM((tm + 2 * U_HALO, d), BF16),
            pltpu.VMEM((CONV_WIDTH // LANES, tm + 2 * U_HALO, LANES), F32),
            pltpu.VMEM((tm, CONV_WIDTH), F32),
        ],
        compiler_params=pltpu.CompilerParams(
            dimension_semantics=("arbitrary", "arbitrary"), vmem_limit_bytes=VMEM_LIMIT),
        name="in_proj",
    )(x, x, x, shift, scale, cos_t, sin_t, w_in, conv_w, conv_b, cln_g, cln_b)


def _mixer_body(sink_ref, act_ref,
                kT_ref, kT_prev_ref, kT_next_ref, v_ref, v_prev_ref, v_next_ref,
                kcT_ref, vc_ref, band_ref, w_out_ref,
                x_ref, gate_ref, pln_g_ref, pln_b_ref,
                o_ref,
                v4_ref, kcT4_ref, vc4_ref, mixed_ref, p_ref, esink_ref, kTw_ref, bias_ref):
    tq = x_ref.shape[0]
    i = pl.program_id(1)
    is_first = i == 0
    is_last = i == pl.num_programs(1) - 1
    n_qblk = tq // QBLK

    n_ctx_tiles = kcT_ref.shape[-1] // LANES
    n_loc_tiles = (QBLK + 2 * WINDOW) // LANES
    loc_cols = n_loc_tiles * LANES
    n_tiles = n_loc_tiles + n_ctx_tiles

    @pl.when(is_first)
    def _():
        for h in range(N_KV_HEADS):
            for g in range(GQA_GROUP):
                kcT4_ref[h, g * HEAD_DIM:(g + 1) * HEAD_DIM, :] = kcT_ref[h * HEAD_DIM:(h + 1) * HEAD_DIM, :]
            vc4_ref[h, :, 0:LANES] = vc_ref[:, h * LANES:(h + 1) * LANES]
            vc4_ref[h, :, LANES:] = jnp.ones((vc4_ref.shape[1], LANES), BF16)
            v4_ref[h, :, LANES:] = jnp.ones((v4_ref.shape[1], LANES), BF16)

    for h in range(N_KV_HEADS):
        lanes = slice(h * LANES, (h + 1) * LANES)
        v4_ref[h, 0:WINDOW, 0:LANES] = v_prev_ref[:, lanes]
        v4_ref[h, WINDOW:WINDOW + tq, 0:LANES] = v_ref[:, lanes]
        v4_ref[h, WINDOW + tq:, 0:LANES] = v_next_ref[:, lanes]

    def head_lanes(g):
        lane = lax.broadcasted_iota(jnp.int32, (1, GQA_GROUP * HEAD_DIM), 1)
        return (lane >= g * HEAD_DIM) & (lane < (g + 1) * HEAD_DIM)

    for j in range(n_qblk):
        for t in range(n_loc_tiles):
            c0 = j * QBLK - WINDOW + t * LANES
            for h in range(N_KV_HEADS):
                rows = slice(h * HEAD_DIM, (h + 1) * HEAD_DIM)
                if c0 < 0:
                    src = kT_prev_ref[rows, :]
                elif c0 >= tq:
                    src = kT_next_ref[rows, :]
                else:
                    src = kT_ref[rows, c0:c0 + LANES]
                for g in range(GQA_GROUP):
                    kTw_ref[h, j, g * HEAD_DIM:(g + 1) * HEAD_DIM, t * LANES:(t + 1) * LANES] = src
        lo = band_ref[:, 0:WINDOW]
        hi = band_ref[:, QBLK + WINDOW:]
        if j == 0:
            lo = lo + jnp.where(is_first, NEG_INF, 0.0)
        if j == n_qblk - 1:
            hi = hi + jnp.where(is_last, NEG_INF, 0.0)
        bias_ref[j, :, 0:WINDOW] = lo
        bias_ref[j, :, WINDOW:] = hi

    mixed_ref[:, ATTN_WIDTH:] = act_ref[:, ACT_C:ACT_C + CONV_WIDTH]

    for j in range(n_qblk):
        q0 = j * QBLK
        for h in range(N_KV_HEADS):
            hl = slice(h * GQA_GROUP * HEAD_DIM, (h + 1) * GQA_GROUP * HEAD_DIM)
            qb = act_ref[q0:q0 + QBLK, ACT_Q + hl.start:ACT_Q + hl.stop]
            lhs = jnp.concatenate(
                [jnp.where(head_lanes(g), qb, jnp.zeros_like(qb)) for g in range(GQA_GROUP)], axis=0)
            s_loc = jnp.dot(lhs, kTw_ref[h, j], preferred_element_type=F32)
            s_ctx = jnp.dot(lhs, kcT4_ref[h], preferred_element_type=F32)
            for rc in range(0, GQA_GROUP * QBLK, SM_ROWS):
                g, rr = divmod(rc, QBLK)
                sink = sink_ref[h * GQA_GROUP + g] * LOG2E
                tiles = [s_loc[rc:rc + SM_ROWS, t * LANES:(t + 1) * LANES] for t in range(n_loc_tiles)]
                tiles += [s_ctx[rc:rc + SM_ROWS, t * LANES:(t + 1) * LANES] for t in range(n_ctx_tiles)]
                tiles[0] = tiles[0] + bias_ref[j, rr:rr + SM_ROWS, 0:WINDOW]
                tiles[n_loc_tiles - 1] = tiles[n_loc_tiles - 1] + bias_ref[j, rr:rr + SM_ROWS, WINDOW:]
                m = jnp.max(functools.reduce(jnp.maximum, tiles), axis=-1, keepdims=True)
                m = jnp.maximum(m, sink)
                esink_ref[h, rc:rc + SM_ROWS, :] = jnp.exp2(sink - m)
                for t in range(n_tiles):
                    p_ref[h, rc:rc + SM_ROWS, t * LANES:(t + 1) * LANES] = jnp.exp2(tiles[t] - m).astype(BF16)
            r = (jnp.dot(p_ref[h, :, 0:loc_cols], v4_ref[h, q0:q0 + loc_cols, :], preferred_element_type=F32)
                 + jnp.dot(p_ref[h, :, loc_cols:], vc4_ref[h], preferred_element_type=F32))
            out = r[:, 0:LANES] * (1.0 / (r[:, LANES:] + esink_ref[h]))
            first_of_pair = lax.broadcasted_iota(jnp.int32, (1, LANES), 1) < HEAD_DIM
            attn = jnp.concatenate(
                [jnp.where(first_of_pair, out[(2 * k) * QBLK:(2 * k + 1) * QBLK],
                           out[(2 * k + 1) * QBLK:(2 * k + 2) * QBLK]) for k in range(GQA_GROUP // 2)], axis=1)
            sag = act_ref[q0:q0 + QBLK, ACT_SAG + hl.start:ACT_SAG + hl.stop].astype(F32)
            mixed_ref[q0:q0 + QBLK, hl] = (attn * sag).astype(BF16)

    y = jnp.dot(mixed_ref[...], w_out_ref[...], preferred_element_type=F32)
    z = x_ref[...] + (gate_ref[...] * (1.0 / ALPHA)) * y
    o_ref[...] = _norm_rows(z, LN_EPS / ALPHA ** 2) * pln_g_ref[...] + pln_b_ref[...]


def _mixer(sink, act, kT, vv, kcT, vc, band, w_out, x, gate, pln_g, pln_b):
    b, n, d = x.shape
    tq = MIXER_ROWS
    m = kcT.shape[-1]
    qpt = tq // QBLK
    n_q = n // QBLK
    const2 = lambda bi, i: (0, 0)
    return pl.pallas_call(
        _mixer_body,
        grid=(b, n // tq),
        in_specs=[
            pl.BlockSpec(memory_space=pltpu.SMEM),
            pl.BlockSpec((None, tq, ACT_WIDTH), lambda bi, i: (bi, i, 0)),
            pl.BlockSpec((None, KV_WIDTH, tq), lambda bi, i: (bi, 0, i)),
            pl.BlockSpec((None, KV_WIDTH, WINDOW), lambda bi, i: (bi, 0, jnp.maximum(i * qpt - 1, 0))),
            pl.BlockSpec((None, KV_WIDTH, WINDOW), lambda bi, i: (bi, 0, jnp.minimum((i + 1) * qpt, n_q - 1))),
            pl.BlockSpec((None, tq, 2 * LANES), lambda bi, i: (bi, i, 0)),
            pl.BlockSpec((None, WINDOW, 2 * LANES), lambda bi, i: (bi, jnp.maximum(i * qpt - 1, 0), 0)),
            pl.BlockSpec((None, WINDOW, 2 * LANES), lambda bi, i: (bi, jnp.minimum((i + 1) * qpt, n_q - 1), 0)),
            pl.BlockSpec((None, KV_WIDTH, m), lambda bi, i: (bi, 0, 0)),
            pl.BlockSpec((None, m, 2 * LANES), lambda bi, i: (bi, 0, 0)),
            pl.BlockSpec((QBLK, QBLK + 2 * WINDOW), const2),
            pl.BlockSpec((d, d), const2),
            pl.BlockSpec((None, tq, d), lambda bi, i: (bi, i, 0)),
            pl.BlockSpec((None, 1, d), lambda bi, i: (bi, 0, 0)),
            pl.BlockSpec((1, d), const2),
            pl.BlockSpec((1, d), const2),
        ],
        out_specs=pl.BlockSpec((None, tq, d), lambda bi, i: (bi, i, 0)),
        out_shape=jax.ShapeDtypeStruct((b, n, d), F32),
        scratch_shapes=[
            pltpu.VMEM((N_KV_HEADS, tq + 2 * WINDOW, 2 * LANES), BF16),
            pltpu.VMEM((N_KV_HEADS, GQA_GROUP * HEAD_DIM, m), BF16),
            pltpu.VMEM((N_KV_HEADS, m, 2 * LANES), BF16),
            pltpu.VMEM((tq, d), BF16),
            pltpu.VMEM((N_KV_HEADS, GQA_GROUP * QBLK, QBLK + 2 * WINDOW + m), BF16),
            pltpu.VMEM((N_KV_HEADS, GQA_GROUP * QBLK, 1), F32),
            pltpu.VMEM((N_KV_HEADS, tq // QBLK, GQA_GROUP * HEAD_DIM, QBLK + 2 * WINDOW), BF16),
            pltpu.VMEM((tq // QBLK, QBLK, 2 * WINDOW), F32),
        ],
        compiler_params=pltpu.CompilerParams(
            dimension_semantics=("arbitrary", "arbitrary"), vmem_limit_bytes=VMEM_LIMIT),
        name="mixer",
    )(sink, act, kT, kT, kT, vv, vv, vv, kcT, vc, band, w_out, x, gate, pln_g, pln_b)


def _rope_tables(n):
    rows = n // GRID_W
    row = jnp.repeat(jnp.arange(rows, dtype=F32), GRID_W)
    colp = jnp.tile(jnp.arange(GRID_W, dtype=F32), rows)
    inv_freq = ROPE_BASE ** (-jnp.arange(0, ROT_AXIS_DIM, 2, dtype=F32) / ROT_AXIS_DIM)
    ang_row = row[:, None] * inv_freq
    ang_col = colp[:, None] * inv_freq
    cos_h = jnp.concatenate([jnp.cos(ang_row), jnp.cos(ang_row), jnp.cos(ang_col), jnp.cos(ang_col)], axis=-1)
    sin_h = jnp.concatenate([-jnp.sin(ang_row), jnp.sin(ang_row), -jnp.sin(ang_col), jnp.sin(ang_col)], axis=-1)
    reps = LANES // HEAD_DIM
    return jnp.tile(cos_h, (1, reps)), jnp.tile(sin_h, (1, reps))


def _band_bias():
    rel = jnp.arange(QBLK + 2 * WINDOW)[None, :] - jnp.arange(QBLK)[:, None]
    return jnp.where((rel >= 0) & (rel <= 2 * WINDOW), 0.0, NEG_INF).astype(F32)


def kernel(x, c, ctx, c_ctx, w_ada, b_ada, w_in, attn_sink, conv_w, conv_b, conv_ln_g, conv_ln_b,
           w_out, post_ln_g, post_ln_b):
    assert w_ada.shape[0] == DEPTH
    b, n, d = x.shape
    cond = jnp.zeros((ADA_ROWS, d), F32).at[:b].set(c).at[b].set(c_ctx)
    mods = _adaln(cond, w_ada[0], b_ada[0][None, :])
    shift, scale, gate = (mods[:b, k * d:(k + 1) * d][:, None, :] for k in range(3))
    shift_c, scale_c = (mods[b:b + 1, k * d:(k + 1) * d] for k in range(2))

    w_in_b = w_in[0].astype(BF16)
    kcT, vc = _ctx_kv(ctx, shift_c, scale_c, w_in_b[:, K_OFF:AG_OFF])
    cos_t, sin_t = _rope_tables(n)
    conv_w_rep = jnp.broadcast_to(conv_w[0][:, None, :], (CONV_SIZE, SUBLANES, CONV_WIDTH))
    act, kT, vv = _in_proj(x, shift, scale, cos_t, sin_t, w_in_b,
                           conv_w_rep, conv_b[0][None, :], conv_ln_g[0][None, :], conv_ln_b[0][None, :])
    return _mixer(attn_sink[0], act, kT, vv, kcT, vc, _band_bias(),
                  w_out[0].astype(BF16), x, gate, post_ln_g[0][None, :], post_ln_b[0][None, :])
```

```python
import functools
import math

import jax
import jax.numpy as jnp
from jax import lax
from jax.experimental import pallas as pl
from jax.experimental.pallas import tpu as pltpu

F32 = jnp.float32
BF16 = jnp.bfloat16

D_MODEL = 1024
GRID_W = 64
HEAD_DIM = 64
N_Q_HEADS = 8
N_KV_HEADS = 2
GQA_GROUP = N_Q_HEADS // N_KV_HEADS
ATTN_WIDTH = N_Q_HEADS * HEAD_DIM
KV_WIDTH = N_KV_HEADS * HEAD_DIM
CONV_WIDTH = D_MODEL - ATTN_WIDTH
CONV_SIZE = 31
CONV_HALF = CONV_SIZE // 2
WINDOW = 128
QBLK = 128
ROPE_BASE = 10000.0
ROT_AXIS_DIM = HEAD_DIM // 2
ROT_HALF = ROT_AXIS_DIM // 2
LN_EPS = 1e-6
NEG_INF = -1e30
DEPTH = 1
ALPHA = (2.0 * DEPTH) ** 0.25

Q_OFF = 0
K_OFF = Q_OFF + ATTN_WIDTH
V_OFF = K_OFF + KV_WIDTH
AG_OFF = V_OFF + KV_WIDTH
CA_OFF = AG_OFF + ATTN_WIDTH
CB_OFF = CA_OFF + CONV_WIDTH
CG_OFF = CB_OFF + CONV_WIDTH
IN_WIDTH = CG_OFF + CONV_WIDTH

ACT_Q = 0
ACT_C = ATTN_WIDTH
ACT_SAG = ACT_C + CONV_WIDTH
ACT_WIDTH = ACT_SAG + ATTN_WIDTH

LANES = 128
SUBLANES = 8
CONV_ROWS = 32
CONV_COLS = 256
SM_ROWS = 32
CONV_PER_PROJ = 4
LOG2E = math.log2(math.e)
U_HALO = 16
TILE_ROWS = 512
MIXER_ROWS = 1024
ADA_ROWS = 40
ADA_COLS = 512
CTX_BATCH = 4
VMEM_LIMIT = 56 * 1024 * 1024


def _norm_rows(x, eps=LN_EPS):
    mu = jnp.mean(x, axis=-1, keepdims=True)
    xc = x - mu
    var = jnp.mean(xc * xc, axis=-1, keepdims=True)
    return xc * lax.rsqrt(var + eps)


def _sigmoid(x):
    return 0.5 + 0.5 * jnp.tanh(0.5 * x)


def _silu(x):
    hx = 0.5 * x
    return hx + hx * jnp.tanh(hx)


def _adaln_body(c_ref, w_ref, b_ref, o_ref):
    o_ref[...] = jnp.dot(_silu(c_ref[...]), w_ref[...], preferred_element_type=F32) + b_ref[...]


def _adaln(cond, w_ada, b_ada):
    n_cols = w_ada.shape[1]
    return pl.pallas_call(
        _adaln_body,
        grid=(n_cols // ADA_COLS,),
        in_specs=[
            pl.BlockSpec((ADA_ROWS, D_MODEL), lambda j: (0, 0)),
            pl.BlockSpec((D_MODEL, ADA_COLS), lambda j: (0, j)),
            pl.BlockSpec((1, ADA_COLS), lambda j: (0, j)),
        ],
        out_specs=pl.BlockSpec((ADA_ROWS, ADA_COLS), lambda j: (0, j)),
        out_shape=jax.ShapeDtypeStruct((ADA_ROWS, n_cols), F32),
        compiler_params=pltpu.CompilerParams(dimension_semantics=("arbitrary",)),
        name="adaln",
    )(cond, w_ada, b_ada)


def _dup_heads(v):
    lane = lax.broadcasted_iota(jnp.int32, (1, LANES), 1)
    swapped = pltpu.roll(v, HEAD_DIM, 1)
    low = lane < HEAD_DIM
    return jnp.where(low, v, swapped), jnp.where(low, swapped, v)


def _ctx_kv_body(ctx_ref, shift_ref, scale_ref, w_ref, kT_ref, v_ref):
    nb, m, d = ctx_ref.shape
    h = _norm_rows(ctx_ref[...].reshape(nb * m, d)) * (1.0 + scale_ref[...]) + shift_ref[...]
    kv = jnp.dot(h.astype(BF16), w_ref[...], preferred_element_type=F32)
    for bi in range(nb):
        rows = slice(bi * m, (bi + 1) * m)
        kT_ref[bi] = kv[rows, :KV_WIDTH].T.astype(BF16)
        v0, v1 = _dup_heads(kv[rows, KV_WIDTH:])
        v_ref[bi, :, :LANES] = v0.astype(BF16)
        v_ref[bi, :, LANES:] = v1.astype(BF16)


def _ctx_kv(ctx, shift_c, scale_c, w_kv):
    b, m, d = ctx.shape
    nb = CTX_BATCH
    return pl.pallas_call(
        _ctx_kv_body,
        grid=(b // nb,),
        in_specs=[
            pl.BlockSpec((nb, m, d), lambda i: (i, 0, 0)),
            pl.BlockSpec((1, d), lambda i: (0, 0)),
            pl.BlockSpec((1, d), lambda i: (0, 0)),
            pl.BlockSpec((d, 2 * KV_WIDTH), lambda i: (0, 0)),
        ],
        out_specs=[
            pl.BlockSpec((nb, KV_WIDTH, m), lambda i: (i, 0, 0)),
            pl.BlockSpec((nb, m, 2 * LANES), lambda i: (i, 0, 0)),
        ],
        out_shape=[
            jax.ShapeDtypeStruct((b, KV_WIDTH, m), BF16),
            jax.ShapeDtypeStruct((b, m, 2 * LANES), BF16),
        ],
        compiler_params=pltpu.CompilerParams(dimension_semantics=("arbitrary",)),
        name="ctx_kv",
    )(ctx, shift_c, scale_c, w_kv)


def _in_proj_body(x_ref, x_prev_ref, x_next_ref, shift_ref, scale_ref, cos_ref, sin_ref, w_ref,
                  conv_w_ref, conv_b_ref, cln_g_ref, cln_b_ref,
                  act_ref, kT_ref, v_ref,
                  hbuf_ref, ubuf_ref, scg_ref):
    tm = x_ref.shape[0]
    i = pl.program_id(1)
    is_first = i == 0
    is_last = i == pl.num_programs(1) - 1

    def modulate(xv):
        return (_norm_rows(xv) * (1.0 + scale_ref[...]) + shift_ref[...]).astype(BF16)

    hbuf_ref[0:U_HALO, :] = modulate(x_prev_ref[...])
    hbuf_ref[U_HALO:U_HALO + tm, :] = modulate(x_ref[...])
    hbuf_ref[U_HALO + tm:, :] = modulate(x_next_ref[...])
    main = slice(U_HALO, U_HALO + tm)

    cos = cos_ref[...]
    sin = sin_ref[...]
    lane = lax.broadcasted_iota(jnp.int32, (1, LANES), 1)
    first_half = (lane % ROT_AXIS_DIM) < ROT_HALF

    def rope(t):
        partner = jnp.where(first_half, pltpu.roll(t, LANES - ROT_HALF, 1), pltpu.roll(t, ROT_HALF, 1))
        return t * cos + partner * sin

    def proj(rows, lo, width):
        return jnp.dot(hbuf_ref[rows, :], w_ref[:, lo:lo + width], preferred_element_type=F32)

    glu = proj(slice(None), CA_OFF, 2 * CONV_WIDTH)
    u = glu[:, :CONV_WIDTH] * _sigmoid(glu[:, CONV_WIDTH:])
    for c in range(CONV_WIDTH // LANES):
        cl = slice(c * LANES, (c + 1) * LANES)
        ubuf_ref[c, 0:U_HALO, :] = jnp.where(is_first, 0.0, u[0:U_HALO, cl])
        ubuf_ref[c, U_HALO:U_HALO + tm, :] = u[U_HALO:U_HALO + tm, cl]
        ubuf_ref[c, U_HALO + tm:, :] = jnp.where(is_last, 0.0, u[U_HALO + tm:, cl])
    scg_ref[...] = _silu(proj(main, CG_OFF, CONV_WIDTH))

    def projection_work():
        for c2 in range(ATTN_WIDTH // (2 * LANES)):
            q = proj(main, Q_OFF + c2 * 2 * LANES, 2 * LANES)
            for c in range(2):
                dst = ACT_Q + (2 * c2 + c) * LANES
                act_ref[:, dst:dst + LANES] = (
                    rope(q[:, c * LANES:(c + 1) * LANES]) * (HEAD_DIM ** -0.5 * LOG2E)).astype(BF16)
            yield
        kv = proj(main, K_OFF, 2 * KV_WIDTH)
        kT_ref[...] = rope(kv[:, :KV_WIDTH]).T.astype(BF16)
        v0, v1 = _dup_heads(kv[:, KV_WIDTH:])
        v_ref[:, :LANES] = v0.astype(BF16)
        v_ref[:, LANES:] = v1.astype(BF16)
        yield
        for c2 in range(ATTN_WIDTH // (2 * LANES)):
            lo = c2 * 2 * LANES
            act_ref[:, ACT_SAG + lo:ACT_SAG + lo + 2 * LANES] = _silu(proj(main, AG_OFF + lo, 2 * LANES)).astype(BF16)
            yield

    def conv_work():
        groups = CONV_ROWS // SUBLANES
        for r0 in range(0, tm, CONV_ROWS):
            slabs = []
            for cb in range(CONV_WIDTH // CONV_COLS):
                ch = slice(cb * CONV_COLS, (cb + 1) * CONV_COLS)
                acc = jnp.zeros((groups, SUBLANES, CONV_COLS), F32) + conv_b_ref[:, ch]
                for t in range(CONV_SIZE):
                    top = r0 + U_HALO - CONV_HALF + t
                    rows = jnp.concatenate(
                        [ubuf_ref[c, top:top + CONV_ROWS, :]
                         for c in range(cb * CONV_COLS // LANES, (cb + 1) * CONV_COLS // LANES)], axis=1)
                    acc = acc + rows.reshape(groups, SUBLANES, CONV_COLS) * conv_w_ref[t, :, ch][None]
                slabs.append(acc.reshape(CONV_ROWS, CONV_COLS))
                yield
            cn = _silu(_norm_rows(jnp.concatenate(slabs, axis=1)) * cln_g_ref[...] + cln_b_ref[...])
            act_ref[r0:r0 + CONV_ROWS, ACT_C:ACT_C + CONV_WIDTH] = (
                cn * scg_ref[r0:r0 + CONV_ROWS, :]).astype(BF16)
            yield

    conv, projections = conv_work(), projection_work()
    conv_left = proj_left = True
    while conv_left or proj_left:
        for _ in range(CONV_PER_PROJ):
            conv_left = conv_left and next(conv, "done") != "done"
        proj_left = proj_left and next(projections, "done") != "done"


def _in_proj(x, shift, scale, cos_t, sin_t, w_in, conv_w, conv_b, cln_g, cln_b):
    b, n, d = x.shape
    tm = TILE_ROWS
    hpt = tm // U_HALO
    n_h = n // U_HALO
    const2 = lambda bi, i: (0, 0)
    return pl.pallas_call(
        _in_proj_body,
        grid=(b, n // tm),
        in_specs=[
            pl.BlockSpec((None, tm, d), lambda bi, i: (bi, i, 0)),
            pl.BlockSpec((None, U_HALO, d), lambda bi, i: (bi, jnp.maximum(i * hpt - 1, 0), 0)),
            pl.BlockSpec((None, U_HALO, d), lambda bi, i: (bi, jnp.minimum((i + 1) * hpt, n_h - 1), 0)),
            pl.BlockSpec((None, 1, d), lambda bi, i: (bi, 0, 0)),
            pl.BlockSpec((None, 1, d), lambda bi, i: (bi, 0, 0)),
            pl.BlockSpec((tm, LANES), lambda bi, i: (i, 0)),
            pl.BlockSpec((tm, LANES), lambda bi, i: (i, 0)),
            pl.BlockSpec((d, IN_WIDTH), const2),
            pl.BlockSpec((CONV_SIZE, SUBLANES, CONV_WIDTH), lambda bi, i: (0, 0, 0)),
            pl.BlockSpec((1, CONV_WIDTH), const2),
            pl.BlockSpec((1, CONV_WIDTH), const2),
            pl.BlockSpec((1, CONV_WIDTH), const2),
        ],
        out_specs=[
            pl.BlockSpec((None, tm, ACT_WIDTH), lambda bi, i: (bi, i, 0)),
            pl.BlockSpec((None, KV_WIDTH, tm), lambda bi, i: (bi, 0, i)),
            pl.BlockSpec((None, tm, 2 * LANES), lambda bi, i: (bi, i, 0)),
        ],
        out_shape=[
            jax.ShapeDtypeStruct((b, n, ACT_WIDTH), BF16),
            jax.ShapeDtypeStruct((b, KV_WIDTH, n), BF16),
            jax.ShapeDtypeStruct((b, n, 2 * LANES), BF16),
        ],
        scratch_shapes=[
            pltpu.VMEM((tm + 2 * U_HALO, d), BF16),
            pltpu.VMEM((CONV_WIDTH // LANES, tm + 2 * U_HALO, LANES), F32),
            pltpu.VMEM((tm, CONV_WIDTH), F32),
        ],
        compiler_params=pltpu.CompilerParams(
            dimension_semantics=("arbitrary", "arbitrary"), vmem_limit_bytes=VMEM_LIMIT),
        name="in_proj",
    )(x, x, x, shift, scale, cos_t, sin_t, w_in, conv_w, conv_b, cln_g, cln_b)


def _mixer_body(sink_ref, act_ref,
                kT_ref, kT_prev_ref, kT_next_ref, v_ref, v_prev_ref, v_next_ref,
                kcT_ref, vc_ref, band_ref, w_out_ref,
                x_ref, gate_ref, pln_g_ref, pln_b_ref,
                o_ref,
                v4_ref, kcT4_ref, vc4_ref, mixed_ref, p_ref, esink_ref, kTw_ref, bias_ref):
    tq = x_ref.shape[0]
    i = pl.program_id(1)
    is_first = i == 0
    is_last = i == pl.num_programs(1) - 1
    n_qblk = tq // QBLK

    n_ctx_tiles = kcT_ref.shape[-1] // LANES
    n_loc_tiles = (QBLK + 2 * WINDOW) // LANES
    loc_cols = n_loc_tiles * LANES
    n_tiles = n_loc_tiles + n_ctx_tiles

    @pl.when(is_first)
    def _():
        for h in range(N_KV_HEADS):
            for g in range(GQA_GROUP):
                kcT4_ref[h, g * HEAD_DIM:(g + 1) * HEAD_DIM, :] = kcT_ref[h * HEAD_DIM:(h + 1) * HEAD_DIM, :]
            vc4_ref[h, :, 0:LANES] = vc_ref[:, h * LANES:(h + 1) * LANES]
            vc4_ref[h, :, LANES:] = jnp.ones((vc4_ref.shape[1], LANES), BF16)
            v4_ref[h, :, LANES:] = jnp.ones((v4_ref.shape[1], LANES), BF16)

    for h in range(N_KV_HEADS):
        lanes = slice(h * LANES, (h + 1) * LANES)
        v4_ref[h, 0:WINDOW, 0:LANES] = v_prev_ref[:, lanes]
        v4_ref[h, WINDOW:WINDOW + tq, 0:LANES] = v_ref[:, lanes]
        v4_ref[h, WINDOW + tq:, 0:LANES] = v_next_ref[:, lanes]

    def head_lanes(g):
        lane = lax.broadcasted_iota(jnp.int32, (1, GQA_GROUP * HEAD_DIM), 1)
        return (lane >= g * HEAD_DIM) & (lane < (g + 1) * HEAD_DIM)

    for j in range(n_qblk):
        for t in range(n_loc_tiles):
            c0 = j * QBLK - WINDOW + t * LANES
            for h in range(N_KV_HEADS):
                rows = slice(h * HEAD_DIM, (h + 1) * HEAD_DIM)
                if c0 < 0:
                    src = kT_prev_ref[rows, :]
                elif c0 >= tq:
                    src = kT_next_ref[rows, :]
                else:
                    src = kT_ref[rows, c0:c0 + LANES]
                for g in range(GQA_GROUP):
                    kTw_ref[h, j, g * HEAD_DIM:(g + 1) * HEAD_DIM, t * LANES:(t + 1) * LANES] = src
        lo = band_ref[:, 0:WINDOW]
        hi = band_ref[:, QBLK + WINDOW:]
        if j == 0:
            lo = lo + jnp.where(is_first, NEG_INF, 0.0)
        if j == n_qblk - 1:
            hi = hi + jnp.where(is_last, NEG_INF, 0.0)
        bias_ref[j, :, 0:WINDOW] = lo
        bias_ref[j, :, WINDOW:] = hi

    mixed_ref[:, ATTN_WIDTH:] = act_ref[:, ACT_C:ACT_C + CONV_WIDTH]

    for j in range(n_qblk):
        q0 = j * QBLK
        for h in range(N_KV_HEADS):
            hl = slice(h * GQA_GROUP * HEAD_DIM, (h + 1) * GQA_GROUP * HEAD_DIM)
            qb = act_ref[q0:q0 + QBLK, ACT_Q + hl.start:ACT_Q + hl.stop]
            lhs = jnp.concatenate(
                [jnp.where(head_lanes(g), qb, jnp.zeros_like(qb)) for g in range(GQA_GROUP)], axis=0)
            s_loc = jnp.dot(lhs, kTw_ref[h, j], preferred_element_type=F32)
            s_ctx = jnp.dot(lhs, kcT4_ref[h], preferred_element_type=F32)
            for rc in range(0, GQA_GROUP * QBLK, SM_ROWS):
                g, rr = divmod(rc, QBLK)
                sink = sink_ref[h * GQA_GROUP + g] * LOG2E
                tiles = [s_loc[rc:rc + SM_ROWS, t * LANES:(t + 1) * LANES] for t in range(n_loc_tiles)]
                tiles += [s_ctx[rc:rc + SM_ROWS, t * LANES:(t + 1) * LANES] for t in range(n_ctx_tiles)]
                tiles[0] = tiles[0] + bias_ref[j, rr:rr + SM_ROWS, 0:WINDOW]
                tiles[n_loc_tiles - 1] = tiles[n_loc_tiles - 1] + bias_ref[j, rr:rr + SM_ROWS, WINDOW:]
                m = jnp.max(functools.reduce(jnp.maximum, tiles), axis=-1, keepdims=True)
                m = jnp.maximum(m, sink)
                esink_ref[h, rc:rc + SM_ROWS, :] = jnp.exp2(sink - m)
                for t in range(n_tiles):
                    p_ref[h, rc:rc + SM_ROWS, t * LANES:(t + 1) * LANES] = jnp.exp2(tiles[t] - m).astype(BF16)
            r = (jnp.dot(p_ref[h, :, 0:loc_cols], v4_ref[h, q0:q0 + loc_cols, :], preferred_element_type=F32)
                 + jnp.dot(p_ref[h, :, loc_cols:], vc4_ref[h], preferred_element_type=F32))
            out = r[:, 0:LANES] * (1.0 / (r[:, LANES:] + esink_ref[h]))
            first_of_pair = lax.broadcasted_iota(jnp.int32, (1, LANES), 1) < HEAD_DIM
            attn = jnp.concatenate(
                [jnp.where(first_of_pair, out[(2 * k) * QBLK:(2 * k + 1) * QBLK],
                           out[(2 * k + 1) * QBLK:(2 * k + 2) * QBLK]) for k in range(GQA_GROUP // 2)], axis=1)
            sag = act_ref[q0:q0 + QBLK, ACT_SAG + hl.start:ACT_SAG + hl.stop].astype(F32)
            mixed_ref[q0:q0 + QBLK, hl] = (attn * sag).astype(BF16)

    y = jnp.dot(mixed_ref[...], w_out_ref[...], preferred_element_type=F32)
    z = x_ref[...] + (gate_ref[...] * (1.0 / ALPHA)) * y
    o_ref[...] = _norm_rows(z, LN_EPS / ALPHA ** 2) * pln_g_ref[...] + pln_b_ref[...]


def _mixer(sink, act, kT, vv, kcT, vc, band, w_out, x, gate, pln_g, pln_b):
    b, n, d = x.shape
    tq = MIXER_ROWS
    m = kcT.shape[-1]
    qpt = tq // QBLK
    n_q = n // QBLK
    const2 = lambda bi, i: (0, 0)
    return pl.pallas_call(
        _mixer_body,
        grid=(b, n // tq),
        in_specs=[
            pl.BlockSpec(memory_space=pltpu.SMEM),
            pl.BlockSpec((None, tq, ACT_WIDTH), lambda bi, i: (bi, i, 0)),
            pl.BlockSpec((None, KV_WIDTH, tq), lambda bi, i: (bi, 0, i)),
            pl.BlockSpec((None, KV_WIDTH, WINDOW), lambda bi, i: (bi, 0, jnp.maximum(i * qpt - 1, 0))),
            pl.BlockSpec((None, KV_WIDTH, WINDOW), lambda bi, i: (bi, 0, jnp.minimum((i + 1) * qpt, n_q - 1))),
            pl.BlockSpec((None, tq, 2 * LANES), lambda bi, i: (bi, i, 0)),
            pl.BlockSpec((None, WINDOW, 2 * LANES), lambda bi, i: (bi, jnp.maximum(i * qpt - 1, 0), 0)),
            pl.BlockSpec((None, WINDOW, 2 * LANES), lambda bi, i: (bi, jnp.minimum((i + 1) * qpt, n_q - 1), 0)),
            pl.BlockSpec((None, KV_WIDTH, m), lambda bi, i: (bi, 0, 0)),
            pl.BlockSpec((None, m, 2 * LANES), lambda bi, i: (bi, 0, 0)),
            pl.BlockSpec((QBLK, QBLK + 2 * WINDOW), const2),
            pl.BlockSpec((d, d), const2),
            pl.BlockSpec((None, tq, d), lambda bi, i: (bi, i, 0)),
            pl.BlockSpec((None, 1, d), lambda bi, i: (bi, 0, 0)),
            pl.BlockSpec((1, d), const2),
            pl.BlockSpec((1, d), const2),
        ],
        out_specs=pl.BlockSpec((None, tq, d), lambda bi, i: (bi, i, 0)),
        out_shape=jax.ShapeDtypeStruct((b, n, d), F32),
        scratch_shapes=[
            pltpu.VMEM((N_KV_HEADS, tq + 2 * WINDOW, 2 * LANES), BF16),
            pltpu.VMEM((N_KV_HEADS, GQA_GROUP * HEAD_DIM, m), BF16),
            pltpu.VMEM((N_KV_HEADS, m, 2 * LANES), BF16),
            pltpu.VMEM((tq, d), BF16),
            pltpu.VMEM((N_KV_HEADS, GQA_GROUP * QBLK, QBLK + 2 * WINDOW + m), BF16),
            pltpu.VMEM((N_KV_HEADS, GQA_GROUP * QBLK, 1), F32),
            pltpu.VMEM((N_KV_HEADS, tq // QBLK, GQA_GROUP * HEAD_DIM, QBLK + 2 * WINDOW), BF16),
            pltpu.VMEM((tq // QBLK, QBLK, 2 * WINDOW), F32),
        ],
        compiler_params=pltpu.CompilerParams(
            dimension_semantics=("arbitrary", "arbitrary"), vmem_limit_bytes=VMEM_LIMIT),
        name="mixer",
    )(sink, act, kT, kT, kT, vv, vv, vv, kcT, vc, band, w_out, x, gate, pln_g, pln_b)


def _rope_tables(n):
    rows = n // GRID_W
    row = jnp.repeat(jnp.arange(rows, dtype=F32), GRID_W)
    colp = jnp.tile(jnp.arange(GRID_W, dtype=F32), rows)
    inv_freq = ROPE_BASE ** (-jnp.arange(0, ROT_AXIS_DIM, 2, dtype=F32) / ROT_AXIS_DIM)
    ang_row = row[:, None] * inv_freq
    ang_col = colp[:, None] * inv_freq
    cos_h = jnp.concatenate([jnp.cos(ang_row), jnp.cos(ang_row), jnp.cos(ang_col), jnp.cos(ang_col)], axis=-1)
    sin_h = jnp.concatenate([-jnp.sin(ang_row), jnp.sin(ang_row), -jnp.sin(ang_col), jnp.sin(ang_col)], axis=-1)
    reps = LANES // HEAD_DIM
    return jnp.tile(cos_h, (1, reps)), jnp.tile(sin_h, (1, reps))


def _band_bias():
    rel = jnp.arange(QBLK + 2 * WINDOW)[None, :] - jnp.arange(QBLK)[:, None]
    return jnp.where((rel >= 0) & (rel <= 2 * WINDOW), 0.0, NEG_INF).astype(F32)


def kernel(x, c, ctx, c_ctx, w_ada, b_ada, w_in, attn_sink, conv_w, conv_b, conv_ln_g, conv_ln_b,
           w_out, post_ln_g, post_ln_b):
    assert w_ada.shape[0] == DEPTH
    b, n, d = x.shape
    cond = jnp.zeros((ADA_ROWS, d), F32).at[:b].set(c).at[b].set(c_ctx)
    mods = _adaln(cond, w_ada[0], b_ada[0][None, :])
    shift, scale, gate = (mods[:b, k * d:(k + 1) * d][:, None, :] for k in range(3))
    shift_c, scale_c = (mods[b:b + 1, k * d:(k + 1) * d] for k in range(2))

    w_in_b = w_in[0].astype(BF16)
    kcT, vc = _ctx_kv(ctx, shift_c, scale_c, w_in_b[:, K_OFF:AG_OFF])
    cos_t, sin_t = _rope_tables(n)
    conv_w_rep = jnp.broadcast_to(conv_w[0][:, None, :], (CONV_SIZE, SUBLANES, CONV_WIDTH))
    act, kT, vv = _in_proj(x, shift, scale, cos_t, sin_t, w_in_b,
                           conv_w_rep, conv_b[0][None, :], conv_ln_g[0][None, :], conv_ln_b[0][None, :])
    return _mixer(attn_sink[0], act, kT, vv, kcT, vc, _band_bias(),
                  w_out[0].astype(BF16), x, gate, post_ln_g[0][None, :], post_ln_b[0][None, :])
```

```python
import functools
import math

import jax
import jax.numpy as jnp
from jax import lax
from jax.experimental import pallas as pl
from jax.experimental.pallas import tpu as pltpu

F32 = jnp.float32
BF16 = jnp.bfloat16

D_MODEL = 1024
GRID_W = 64
HEAD_DIM = 64
N_Q_HEADS = 8
N_KV_HEADS = 2
GQA_GROUP = N_Q_HEADS // N_KV_HEADS
ATTN_WIDTH = N_Q_HEADS * HEAD_DIM
KV_WIDTH = N_KV_HEADS * HEAD_DIM
CONV_WIDTH = D_MODEL - ATTN_WIDTH
CONV_SIZE = 31
CONV_HALF = CONV_SIZE // 2
WINDOW = 128
QBLK = 128
ROPE_BASE = 10000.0
ROT_AXIS_DIM = HEAD_DIM // 2
ROT_HALF = ROT_AXIS_DIM // 2
LN_EPS = 1e-6
NEG_INF = -1e30
DEPTH = 1
ALPHA = (2.0 * DEPTH) ** 0.25

Q_OFF = 0
K_OFF = Q_OFF + ATTN_WIDTH
V_OFF = K_OFF + KV_WIDTH
AG_OFF = V_OFF + KV_WIDTH
CA_OFF = AG_OFF + ATTN_WIDTH
CB_OFF = CA_OFF + CONV_WIDTH
CG_OFF = CB_OFF + CONV_WIDTH
IN_WIDTH = CG_OFF + CONV_WIDTH

ACT_Q = 0
ACT_C = ATTN_WIDTH
ACT_SAG = ACT_C + CONV_WIDTH
ACT_WIDTH = ACT_SAG + ATTN_WIDTH

LANES = 128
SUBLANES = 8
CONV_ROWS = 32
CONV_COLS = 256
SM_ROWS = 32
CONV_PER_PROJ = 4
LOG2E = math.log2(math.e)
U_HALO = 16
TILE_ROWS = 1024
MIXER_ROWS = 1024
ADA_ROWS = 40
ADA_COLS = 512
CTX_BATCH = 4
VMEM_LIMIT = 56 * 1024 * 1024


def _norm_rows(x, eps=LN_EPS):
    mu = jnp.mean(x, axis=-1, keepdims=True)
    xc = x - mu
    var = jnp.mean(xc * xc, axis=-1, keepdims=True)
    return xc * lax.rsqrt(var + eps)


def _sigmoid(x):
    return 0.5 + 0.5 * jnp.tanh(0.5 * x)


def _silu(x):
    hx = 0.5 * x
    return hx + hx * jnp.tanh(hx)


def _adaln_body(c_ref, w_ref, b_ref, o_ref):
    o_ref[...] = jnp.dot(_silu(c_ref[...]), w_ref[...], preferred_element_type=F32) + b_ref[...]


def _adaln(cond, w_ada, b_ada):
    n_cols = w_ada.shape[1]
    return pl.pallas_call(
        _adaln_body,
        grid=(n_cols // ADA_COLS,),
        in_specs=[
            pl.BlockSpec((ADA_ROWS, D_MODEL), lambda j: (0, 0)),
            pl.BlockSpec((D_MODEL, ADA_COLS), lambda j: (0, j)),
            pl.BlockSpec((1, ADA_COLS), lambda j: (0, j)),
        ],
        out_specs=pl.BlockSpec((ADA_ROWS, ADA_COLS), lambda j: (0, j)),
        out_shape=jax.ShapeDtypeStruct((ADA_ROWS, n_cols), F32),
        compiler_params=pltpu.CompilerParams(dimension_semantics=("arbitrary",)),
        name="adaln",
    )(cond, w_ada, b_ada)


def _dup_heads(v):
    lane = lax.broadcasted_iota(jnp.int32, (1, LANES), 1)
    swapped = pltpu.roll(v, HEAD_DIM, 1)
    low = lane < HEAD_DIM
    return jnp.where(low, v, swapped), jnp.where(low, swapped, v)


def _ctx_kv_body(ctx_ref, shift_ref, scale_ref, w_ref, kT_ref, v_ref):
    nb, m, d = ctx_ref.shape
    h = _norm_rows(ctx_ref[...].reshape(nb * m, d)) * (1.0 + scale_ref[...]) + shift_ref[...]
    kv = jnp.dot(h.astype(BF16), w_ref[...], preferred_element_type=F32)
    for bi in range(nb):
        rows = slice(bi * m, (bi + 1) * m)
        kT_ref[bi] = kv[rows, :KV_WIDTH].T.astype(BF16)
        v0, v1 = _dup_heads(kv[rows, KV_WIDTH:])
        v_ref[bi, :, :LANES] = v0.astype(BF16)
        v_ref[bi, :, LANES:] = v1.astype(BF16)


def _ctx_kv(ctx, shift_c, scale_c, w_kv):
    b, m, d = ctx.shape
    nb = CTX_BATCH
    return pl.pallas_call(
        _ctx_kv_body,
        grid=(b // nb,),
        in_specs=[
            pl.BlockSpec((nb, m, d), lambda i: (i, 0, 0)),
            pl.BlockSpec((1, d), lambda i: (0, 0)),
            pl.BlockSpec((1, d), lambda i: (0, 0)),
            pl.BlockSpec((d, 2 * KV_WIDTH), lambda i: (0, 0)),
        ],
        out_specs=[
            pl.BlockSpec((nb, KV_WIDTH, m), lambda i: (i, 0, 0)),
            pl.BlockSpec((nb, m, 2 * LANES), lambda i: (i, 0, 0)),
        ],
        out_shape=[
            jax.ShapeDtypeStruct((b, KV_WIDTH, m), BF16),
            jax.ShapeDtypeStruct((b, m, 2 * LANES), BF16),
        ],
        compiler_params=pltpu.CompilerParams(dimension_semantics=("arbitrary",)),
        name="ctx_kv",
    )(ctx, shift_c, scale_c, w_kv)


def _in_proj_body(x_ref, x_prev_ref, x_next_ref, shift_ref, scale_ref, cos_ref, sin_ref, w_ref,
                  conv_w_ref, conv_b_ref, cln_g_ref, cln_b_ref,
                  act_ref, kT_ref, v_ref,
                  hbuf_ref, ubuf_ref, scg_ref):
    tm = x_ref.shape[0]
    i = pl.program_id(1)
    is_first = i == 0
    is_last = i == pl.num_programs(1) - 1

    def modulate(xv):
        return (_norm_rows(xv) * (1.0 + scale_ref[...]) + shift_ref[...]).astype(BF16)

    hbuf_ref[0:U_HALO, :] = modulate(x_prev_ref[...])
    hbuf_ref[U_HALO:U_HALO + tm, :] = modulate(x_ref[...])
    hbuf_ref[U_HALO + tm:, :] = modulate(x_next_ref[...])
    main = slice(U_HALO, U_HALO + tm)

    cos = cos_ref[...]
    sin = sin_ref[...]
    lane = lax.broadcasted_iota(jnp.int32, (1, LANES), 1)
    first_half = (lane % ROT_AXIS_DIM) < ROT_HALF

    def rope(t):
        partner = jnp.where(first_half, pltpu.roll(t, LANES - ROT_HALF, 1), pltpu.roll(t, ROT_HALF, 1))
        return t * cos + partner * sin

    def proj(rows, lo, width):
        return jnp.dot(hbuf_ref[rows, :], w_ref[:, lo:lo + width], preferred_element_type=F32)

    glu = proj(slice(None), CA_OFF, 2 * CONV_WIDTH)
    u = glu[:, :CONV_WIDTH] * _sigmoid(glu[:, CONV_WIDTH:])
    for c in range(CONV_WIDTH // LANES):
        cl = slice(c * LANES, (c + 1) * LANES)
        ubuf_ref[c, 0:U_HALO, :] = jnp.where(is_first, 0.0, u[0:U_HALO, cl])
        ubuf_ref[c, U_HALO:U_HALO + tm, :] = u[U_HALO:U_HALO + tm, cl]
        ubuf_ref[c, U_HALO + tm:, :] = jnp.where(is_last, 0.0, u[U_HALO + tm:, cl])
    scg_ref[...] = _silu(proj(main, CG_OFF, CONV_WIDTH))

    def projection_work():
        for c2 in range(ATTN_WIDTH // (2 * LANES)):
            q = proj(main, Q_OFF + c2 * 2 * LANES, 2 * LANES)
            for c in range(2):
                dst = ACT_Q + (2 * c2 + c) * LANES
                act_ref[:, dst:dst + LANES] = (
                    rope(q[:, c * LANES:(c + 1) * LANES]) * (HEAD_DIM ** -0.5 * LOG2E)).astype(BF16)
            yield
        kv = proj(main, K_OFF, 2 * KV_WIDTH)
        kT_ref[...] = rope(kv[:, :KV_WIDTH]).T.astype(BF16)
        v0, v1 = _dup_heads(kv[:, KV_WIDTH:])
        v_ref[:, :LANES] = v0.astype(BF16)
        v_ref[:, LANES:] = v1.astype(BF16)
        yield
        for c2 in range(ATTN_WIDTH // (2 * LANES)):
            lo = c2 * 2 * LANES
            act_ref[:, ACT_SAG + lo:ACT_SAG + lo + 2 * LANES] = _silu(proj(main, AG_OFF + lo, 2 * LANES)).astype(BF16)
            yield

    def conv_work():
        groups = CONV_ROWS // SUBLANES
        for r0 in range(0, tm, CONV_ROWS):
            slabs = []
            for cb in range(CONV_WIDTH // CONV_COLS):
                ch = slice(cb * CONV_COLS, (cb + 1) * CONV_COLS)
                acc = jnp.zeros((groups, SUBLANES, CONV_COLS), F32) + conv_b_ref[:, ch]
                for t in range(CONV_SIZE):
                    top = r0 + U_HALO - CONV_HALF + t
                    rows = jnp.concatenate(
                        [ubuf_ref[c, top:top + CONV_ROWS, :]
                         for c in range(cb * CONV_COLS // LANES, (cb + 1) * CONV_COLS // LANES)], axis=1)
                    acc = acc + rows.reshape(groups, SUBLANES, CONV_COLS) * conv_w_ref[t, :, ch][None]
                slabs.append(acc.reshape(CONV_ROWS, CONV_COLS))
                yield
            cn = _silu(_norm_rows(jnp.concatenate(slabs, axis=1)) * cln_g_ref[...] + cln_b_ref[...])
            act_ref[r0:r0 + CONV_ROWS, ACT_C:ACT_C + CONV_WIDTH] = (
                cn * scg_ref[r0:r0 + CONV_ROWS, :]).astype(BF16)
            yield

    conv, projections = conv_work(), projection_work()
    conv_left = proj_left = True
    while conv_left or proj_left:
        for _ in range(CONV_PER_PROJ):
            conv_left = conv_left and next(conv, "done") != "done"
        proj_left = proj_left and next(projections, "done") != "done"


def _in_proj(x, shift, scale, cos_t, sin_t, w_in, conv_w, conv_b, cln_g, cln_b):
    b, n, d = x.shape
    tm = TILE_ROWS
    hpt = tm // U_HALO
    n_h = n // U_HALO
    const2 = lambda bi, i: (0, 0)
    return pl.pallas_call(
        _in_proj_body,
        grid=(b, n // tm),
        in_specs=[
            pl.BlockSpec((None, tm, d), lambda bi, i: (bi, i, 0)),
            pl.BlockSpec((None, U_HALO, d), lambda bi, i: (bi, jnp.maximum(i * hpt - 1, 0), 0)),
            pl.BlockSpec((None, U_HALO, d), lambda bi, i: (bi, jnp.minimum((i + 1) * hpt, n_h - 1), 0)),
            pl.BlockSpec((None, 1, d), lambda bi, i: (bi, 0, 0)),
            pl.BlockSpec((None, 1, d), lambda bi, i: (bi, 0, 0)),
            pl.BlockSpec((tm, LANES), lambda bi, i: (i, 0)),
            pl.BlockSpec((tm, LANES), lambda bi, i: (i, 0)),
            pl.BlockSpec((d, IN_WIDTH), const2),
            pl.BlockSpec((CONV_SIZE, SUBLANES, CONV_WIDTH), lambda bi, i: (0, 0, 0)),
            pl.BlockSpec((1, CONV_WIDTH), const2),
            pl.BlockSpec((1, CONV_WIDTH), const2),
            pl.BlockSpec((1, CONV_WIDTH), const2),
        ],
        out_specs=[
            pl.BlockSpec((None, tm, ACT_WIDTH), lambda bi, i: (bi, i, 0)),
            pl.BlockSpec((None, KV_WIDTH, tm), lambda bi, i: (bi, 0, i)),
            pl.BlockSpec((None, tm, 2 * LANES), lambda bi, i: (bi, i, 0)),
        ],
        out_shape=[
            jax.ShapeDtypeStruct((b, n, ACT_WIDTH), BF16),
            jax.ShapeDtypeStruct((b, KV_WIDTH, n), BF16),
            jax.ShapeDtypeStruct((b, n, 2 * LANES), BF16),
        ],
        scratch_shapes=[
            pltpu.VMEM((tm + 2 * U_HALO, d), BF16),
            pltpu.VMEM((CONV_WIDTH // LANES, tm + 2 * U_HALO, LANES), F32),
            pltpu.VMEM((tm, CONV_WIDTH), F32),
        ],
        compiler_params=pltpu.CompilerParams(
            dimension_semantics=("arbitrary", "arbitrary"), vmem_limit_bytes=VMEM_LIMIT),
        name="in_proj",
    )(x, x, x, shift, scale, cos_t, sin_t, w_in, conv_w, conv_b, cln_g, cln_b)


def _mixer_body(sink_ref, act_ref,
                kT_ref, kT_prev_ref, kT_next_ref, v_ref, v_prev_ref, v_next_ref,
                kcT_ref, vc_ref, band_ref, w_out_ref,
                x_ref, gate_ref, pln_g_ref, pln_b_ref,
                o_ref,
                v4_ref, kcT4_ref, vc4_ref, mixed_ref, p_ref, esink_ref, kTw_ref, bias_ref):
    tq = x_ref.shape[0]
    i = pl.program_id(1)
    is_first = i == 0
    is_last = i == pl.num_programs(1) - 1
    n_qblk = tq // QBLK

    n_ctx_tiles = kcT_ref.shape[-1] // LANES
    n_loc_tiles = (QBLK + 2 * WINDOW) // LANES
    loc_cols = n_loc_tiles * LANES
    n_tiles = n_loc_tiles + n_ctx_tiles

    @pl.when(is_first)
    def _():
        for h in range(N_KV_HEADS):
            for g in range(GQA_GROUP):
                kcT4_ref[h, g * HEAD_DIM:(g + 1) * HEAD_DIM, :] = kcT_ref[h * HEAD_DIM:(h + 1) * HEAD_DIM, :]
            vc4_ref[h, :, 0:LANES] = vc_ref[:, h * LANES:(h + 1) * LANES]
            vc4_ref[h, :, LANES:] = jnp.ones((vc4_ref.shape[1], LANES), BF16)
            v4_ref[h, :, LANES:] = jnp.ones((v4_ref.shape[1], LANES), BF16)

    for h in range(N_KV_HEADS):
        lanes = slice(h * LANES, (h + 1) * LANES)
        v4_ref[h, 0:WINDOW, 0:LANES] = v_prev_ref[:, lanes]
        v4_ref[h, WINDOW:WINDOW + tq, 0:LANES] = v_ref[:, lanes]
        v4_ref[h, WINDOW + tq:, 0:LANES] = v_next_ref[:, lanes]

    def head_lanes(g):
        lane = lax.broadcasted_iota(jnp.int32, (1, GQA_GROUP * HEAD_DIM), 1)
        return (lane >= g * HEAD_DIM) & (lane < (g + 1) * HEAD_DIM)

    for j in range(n_qblk):
        for t in range(n_loc_tiles):
            c0 = j * QBLK - WINDOW + t * LANES
            for h in range(N_KV_HEADS):
                rows = slice(h * HEAD_DIM, (h + 1) * HEAD_DIM)
                if c0 < 0:
                    src = kT_prev_ref[rows, :]
                elif c0 >= tq:
                    src = kT_next_ref[rows, :]
                else:
                    src = kT_ref[rows, c0:c0 + LANES]
                for g in range(GQA_GROUP):
                    kTw_ref[h, j, g * HEAD_DIM:(g + 1) * HEAD_DIM, t * LANES:(t + 1) * LANES] = src
        lo = band_ref[:, 0:WINDOW]
        hi = band_ref[:, QBLK + WINDOW:]
        if j == 0:
            lo = lo + jnp.where(is_first, NEG_INF, 0.0)
        if j == n_qblk - 1:
            hi = hi + jnp.where(is_last, NEG_INF, 0.0)
        bias_ref[j, :, 0:WINDOW] = lo
        bias_ref[j, :, WINDOW:] = hi

    mixed_ref[:, ATTN_WIDTH:] = act_ref[:, ACT_C:ACT_C + CONV_WIDTH]

    for j in range(n_qblk):
        q0 = j * QBLK
        for h in range(N_KV_HEADS):
            hl = slice(h * GQA_GROUP * HEAD_DIM, (h + 1) * GQA_GROUP * HEAD_DIM)
            qb = act_ref[q0:q0 + QBLK, ACT_Q + hl.start:ACT_Q + hl.stop]
            lhs = jnp.concatenate(
                [jnp.where(head_lanes(g), qb, jnp.zeros_like(qb)) for g in range(GQA_GROUP)], axis=0)
            s_loc = jnp.dot(lhs, kTw_ref[h, j], preferred_element_type=F32)
            s_ctx = jnp.dot(lhs, kcT4_ref[h], preferred_element_type=F32)
            for rc in range(0, GQA_GROUP * QBLK, SM_ROWS):
                g, rr = divmod(rc, QBLK)
                sink = sink_ref[h * GQA_GROUP + g] * LOG2E
                tiles = [s_loc[rc:rc + SM_ROWS, t * LANES:(t + 1) * LANES] for t in range(n_loc_tiles)]
                tiles += [s_ctx[rc:rc + SM_ROWS, t * LANES:(t + 1) * LANES] for t in range(n_ctx_tiles)]
                tiles[0] = tiles[0] + bias_ref[j, rr:rr + SM_ROWS, 0:WINDOW]
                tiles[n_loc_tiles - 1] = tiles[n_loc_tiles - 1] + bias_ref[j, rr:rr + SM_ROWS, WINDOW:]
                m = jnp.max(functools.reduce(jnp.maximum, tiles), axis=-1, keepdims=True)
                m = jnp.maximum(m, sink)
                esink_ref[h, rc:rc + SM_ROWS, :] = jnp.exp2(sink - m)
                for t in range(n_tiles):
                    p_ref[h, rc:rc + SM_ROWS, t * LANES:(t + 1) * LANES] = jnp.exp2(tiles[t] - m).astype(BF16)
            r = (jnp.dot(p_ref[h, :, 0:loc_cols], v4_ref[h, q0:q0 + loc_cols, :], preferred_element_type=F32)
                 + jnp.dot(p_ref[h, :, loc_cols:], vc4_ref[h], preferred_element_type=F32))
            out = r[:, 0:LANES] * (1.0 / (r[:, LANES:] + esink_ref[h]))
            first_of_pair = lax.broadcasted_iota(jnp.int32, (1, LANES), 1) < HEAD_DIM
            attn = jnp.concatenate(
                [jnp.where(first_of_pair, out[(2 * k) * QBLK:(2 * k + 1) * QBLK],
                           out[(2 * k + 1) * QBLK:(2 * k + 2) * QBLK]) for k in range(GQA_GROUP // 2)], axis=1)
            sag = act_ref[q0:q0 + QBLK, ACT_SAG + hl.start:ACT_SAG + hl.stop].astype(F32)
            mixed_ref[q0:q0 + QBLK, hl] = (attn * sag).astype(BF16)

    y = jnp.dot(mixed_ref[...], w_out_ref[...], preferred_element_type=F32)
    z = x_ref[...] + (gate_ref[...] * (1.0 / ALPHA)) * y
    o_ref[...] = _norm_rows(z, LN_EPS / ALPHA ** 2) * pln_g_ref[...] + pln_b_ref[...]


def _mixer(sink, act, kT, vv, kcT, vc, band, w_out, x, gate, pln_g, pln_b):
    b, n, d = x.shape
    tq = MIXER_ROWS
    m = kcT.shape[-1]
    qpt = tq // QBLK
    n_q = n // QBLK
    const2 = lambda bi, i: (0, 0)
    return pl.pallas_call(
        _mixer_body,
        grid=(b, n // tq),
        in_specs=[
            pl.BlockSpec(memory_space=pltpu.SMEM),
            pl.BlockSpec((None, tq, ACT_WIDTH), lambda bi, i: (bi, i, 0)),
            pl.BlockSpec((None, KV_WIDTH, tq), lambda bi, i: (bi, 0, i)),
            pl.BlockSpec((None, KV_WIDTH, WINDOW), lambda bi, i: (bi, 0, jnp.maximum(i * qpt - 1, 0))),
            pl.BlockSpec((None, KV_WIDTH, WINDOW), lambda bi, i: (bi, 0, jnp.minimum((i + 1) * qpt, n_q - 1))),
            pl.BlockSpec((None, tq, 2 * LANES), lambda bi, i: (bi, i, 0)),
            pl.BlockSpec((None, WINDOW, 2 * LANES), lambda bi, i: (bi, jnp.maximum(i * qpt - 1, 0), 0)),
            pl.BlockSpec((None, WINDOW, 2 * LANES), lambda bi, i: (bi, jnp.minimum((i + 1) * qpt, n_q - 1), 0)),
            pl.BlockSpec((None, KV_WIDTH, m), lambda bi, i: (bi, 0, 0)),
            pl.BlockSpec((None, m, 2 * LANES), lambda bi, i: (bi, 0, 0)),
            pl.BlockSpec((QBLK, QBLK + 2 * WINDOW), const2),
            pl.BlockSpec((d, d), const2),
            pl.BlockSpec((None, tq, d), lambda bi, i: (bi, i, 0)),
            pl.BlockSpec((None, 1, d), lambda bi, i: (bi, 0, 0)),
            pl.BlockSpec((1, d), const2),
            pl.BlockSpec((1, d), const2),
        ],
        out_specs=pl.BlockSpec((None, tq, d), lambda bi, i: (bi, i, 0)),
        out_shape=jax.ShapeDtypeStruct((b, n, d), F32),
        scratch_shapes=[
            pltpu.VMEM((N_KV_HEADS, tq + 2 * WINDOW, 2 * LANES), BF16),
            pltpu.VMEM((N_KV_HEADS, GQA_GROUP * HEAD_DIM, m), BF16),
            pltpu.VMEM((N_KV_HEADS, m, 2 * LANES), BF16),
            pltpu.VMEM((tq, d), BF16),
            pltpu.VMEM((N_KV_HEADS, GQA_GROUP * QBLK, QBLK + 2 * WINDOW + m), BF16),
            pltpu.VMEM((N_KV_HEADS, GQA_GROUP * QBLK, 1), F32),
            pltpu.VMEM((N_KV_HEADS, tq // QBLK, GQA_GROUP * HEAD_DIM, QBLK + 2 * WINDOW), BF16),
            pltpu.VMEM((tq // QBLK, QBLK, 2 * WINDOW), F32),
        ],
        compiler_params=pltpu.CompilerParams(
            dimension_semantics=("arbitrary", "arbitrary"), vmem_limit_bytes=VMEM_LIMIT),
        name="mixer",
    )(sink, act, kT, kT, kT, vv, vv, vv, kcT, vc, band, w_out, x, gate, pln_g, pln_b)


def _rope_tables(n):
    rows = n // GRID_W
    row = jnp.repeat(jnp.arange(rows, dtype=F32), GRID_W)
    colp = jnp.tile(jnp.arange(GRID_W, dtype=F32), rows)
    inv_freq = ROPE_BASE ** (-jnp.arange(0, ROT_AXIS_DIM, 2, dtype=F32) / ROT_AXIS_DIM)
    ang_row = row[:, None] * inv_freq
    ang_col = colp[:, None] * inv_freq
    cos_h = jnp.concatenate([jnp.cos(ang_row), jnp.cos(ang_row), jnp.cos(ang_col), jnp.cos(ang_col)], axis=-1)
    sin_h = jnp.concatenate([-jnp.sin(ang_row), jnp.sin(ang_row), -jnp.sin(ang_col), jnp.sin(ang_col)], axis=-1)
    reps = LANES // HEAD_DIM
    return jnp.tile(cos_h, (1, reps)), jnp.tile(sin_h, (1, reps))


def _band_bias():
    rel = jnp.arange(QBLK + 2 * WINDOW)[None, :] - jnp.arange(QBLK)[:, None]
    return jnp.where((rel >= 0) & (rel <= 2 * WINDOW), 0.0, NEG_INF).astype(F32)


def kernel(x, c, ctx, c_ctx, w_ada, b_ada, w_in, attn_sink, conv_w, conv_b, conv_ln_g, conv_ln_b,
           w_out, post_ln_g, post_ln_b):
    assert w_ada.shape[0] == DEPTH
    b, n, d = x.shape
    cond = jnp.zeros((ADA_ROWS, d), F32).at[:b].set(c).at[b].set(c_ctx)
    mods = _adaln(cond, w_ada[0], b_ada[0][None, :])
    shift, scale, gate = (mods[:b, k * d:(k + 1) * d][:, None, :] for k in range(3))
    shift_c, scale_c = (mods[b:b + 1, k * d:(k + 1) * d] for k in range(2))

    w_in_b = w_in[0].astype(BF16)
    kcT, vc = _ctx_kv(ctx, shift_c, scale_c, w_in_b[:, K_OFF:AG_OFF])
    cos_t, sin_t = _rope_tables(n)
    conv_w_rep = jnp.broadcast_to(conv_w[0][:, None, :], (CONV_SIZE, SUBLANES, CONV_WIDTH))
    act, kT, vv = _in_proj(x, shift, scale, cos_t, sin_t, w_in_b,
                           conv_w_rep, conv_b[0][None, :], conv_ln_g[0][None, :], conv_ln_b[0][None, :])
    return _mixer(attn_sink[0], act, kT, vv, kcT, vc, _band_bias(),
                  w_out[0].astype(BF16), x, gate, post_ln_g[0][None, :], post_ln_b[0][None, :])
```

```python
import functools
import math

import jax
import jax.numpy as jnp
from jax import lax
from jax.experimental import pallas as pl
from jax.experimental.pallas import tpu as pltpu

F32 = jnp.float32
BF16 = jnp.bfloat16

D_MODEL = 1024
GRID_W = 64
HEAD_DIM = 64
N_Q_HEADS = 8
N_KV_HEADS = 2
GQA_GROUP = N_Q_HEADS // N_KV_HEADS
ATTN_WIDTH = N_Q_HEADS * HEAD_DIM
KV_WIDTH = N_KV_HEADS * HEAD_DIM
CONV_WIDTH = D_MODEL - ATTN_WIDTH
CONV_SIZE = 31
CONV_HALF = CONV_SIZE // 2
WINDOW = 128
QBLK = 128
ROPE_BASE = 10000.0
ROT_AXIS_DIM = HEAD_DIM // 2
ROT_HALF = ROT_AXIS_DIM // 2
LN_EPS = 1e-6
NEG_INF = -1e30
DEPTH = 1
ALPHA = (2.0 * DEPTH) ** 0.25

Q_OFF = 0
K_OFF = Q_OFF + ATTN_WIDTH
V_OFF = K_OFF + KV_WIDTH
AG_OFF = V_OFF + KV_WIDTH
CA_OFF = AG_OFF + ATTN_WIDTH
CB_OFF = CA_OFF + CONV_WIDTH
CG_OFF = CB_OFF + CONV_WIDTH
IN_WIDTH = CG_OFF + CONV_WIDTH

ACT_Q = 0
ACT_C = ATTN_WIDTH
ACT_SAG = ACT_C + CONV_WIDTH
ACT_WIDTH = ACT_SAG + ATTN_WIDTH

LANES = 128
SUBLANES = 8
CONV_ROWS = 32
CONV_COLS = 256
SM_ROWS = 32
LOG2E = math.log2(math.e)
U_HALO = 16
TILE_ROWS = 1024
MIXER_ROWS = 1024
ADA_ROWS = 40
ADA_COLS = 512
CTX_BATCH = 4
VMEM_LIMIT = 56 * 1024 * 1024


def _norm_rows(x, eps=LN_EPS):
    mu = jnp.mean(x, axis=-1, keepdims=True)
    xc = x - mu
    var = jnp.mean(xc * xc, axis=-1, keepdims=True)
    return xc * lax.rsqrt(var + eps)


def _sigmoid(x):
    return 0.5 + 0.5 * jnp.tanh(0.5 * x)


def _silu(x):
    hx = 0.5 * x
    return hx + hx * jnp.tanh(hx)


def _adaln_body(c_ref, w_ref, b_ref, o_ref):
    o_ref[...] = jnp.dot(_silu(c_ref[...]), w_ref[...], preferred_element_type=F32) + b_ref[...]


def _adaln(cond, w_ada, b_ada):
    n_cols = w_ada.shape[1]
    return pl.pallas_call(
        _adaln_body,
        grid=(n_cols // ADA_COLS,),
        in_specs=[
            pl.BlockSpec((ADA_ROWS, D_MODEL), lambda j: (0, 0)),
            pl.BlockSpec((D_MODEL, ADA_COLS), lambda j: (0, j)),
            pl.BlockSpec((1, ADA_COLS), lambda j: (0, j)),
        ],
        out_specs=pl.BlockSpec((ADA_ROWS, ADA_COLS), lambda j: (0, j)),
        out_shape=jax.ShapeDtypeStruct((ADA_ROWS, n_cols), F32),
        compiler_params=pltpu.CompilerParams(dimension_semantics=("arbitrary",)),
        name="adaln",
    )(cond, w_ada, b_ada)


def _dup_heads(v):
    lane = lax.broadcasted_iota(jnp.int32, (1, LANES), 1)
    swapped = pltpu.roll(v, HEAD_DIM, 1)
    low = lane < HEAD_DIM
    return jnp.where(low, v, swapped), jnp.where(low, swapped, v)


def _ctx_kv_body(ctx_ref, shift_ref, scale_ref, w_ref, kT_ref, v_ref):
    nb, m, d = ctx_ref.shape
    h = _norm_rows(ctx_ref[...].reshape(nb * m, d)) * (1.0 + scale_ref[...]) + shift_ref[...]
    kv = jnp.dot(h.astype(BF16), w_ref[...], preferred_element_type=F32)
    for bi in range(nb):
        rows = slice(bi * m, (bi + 1) * m)
        kT_ref[bi] = kv[rows, :KV_WIDTH].T.astype(BF16)
        v0, v1 = _dup_heads(kv[rows, KV_WIDTH:])
        v_ref[bi, :, :LANES] = v0.astype(BF16)
        v_ref[bi, :, LANES:] = v1.astype(BF16)


def _ctx_kv(ctx, shift_c, scale_c, w_kv):
    b, m, d = ctx.shape
    nb = CTX_BATCH
    return pl.pallas_call(
        _ctx_kv_body,
        grid=(b // nb,),
        in_specs=[
            pl.BlockSpec((nb, m, d), lambda i: (i, 0, 0)),
            pl.BlockSpec((1, d), lambda i: (0, 0)),
            pl.BlockSpec((1, d), lambda i: (0, 0)),
            pl.BlockSpec((d, 2 * KV_WIDTH), lambda i: (0, 0)),
        ],
        out_specs=[
            pl.BlockSpec((nb, KV_WIDTH, m), lambda i: (i, 0, 0)),
            pl.BlockSpec((nb, m, 2 * LANES), lambda i: (i, 0, 0)),
        ],
        out_shape=[
            jax.ShapeDtypeStruct((b, KV_WIDTH, m), BF16),
            jax.ShapeDtypeStruct((b, m, 2 * LANES), BF16),
        ],
        compiler_params=pltpu.CompilerParams(dimension_semantics=("arbitrary",)),
        name="ctx_kv",
    )(ctx, shift_c, scale_c, w_kv)


def _in_proj_body(x_ref, x_prev_ref, x_next_ref, shift_ref, scale_ref, cos_ref, sin_ref, w_ref,
                  conv_w_ref, conv_b_ref, cln_g_ref, cln_b_ref,
                  act_ref, kT_ref, v_ref,
                  hbuf_ref, ubuf_ref, scg_ref):
    tm = x_ref.shape[0]
    i = pl.program_id(1)
    is_first = i == 0
    is_last = i == pl.num_programs(1) - 1

    def modulate(xv):
        return (_norm_rows(xv) * (1.0 + scale_ref[...]) + shift_ref[...]).astype(BF16)

    hbuf_ref[0:U_HALO, :] = modulate(x_prev_ref[...])
    hbuf_ref[U_HALO:U_HALO + tm, :] = modulate(x_ref[...])
    hbuf_ref[U_HALO + tm:, :] = modulate(x_next_ref[...])
    main = slice(U_HALO, U_HALO + tm)

    cos = cos_ref[...]
    sin = sin_ref[...]
    lane = lax.broadcasted_iota(jnp.int32, (1, LANES), 1)
    first_half = (lane % ROT_AXIS_DIM) < ROT_HALF

    def rope(t):
        partner = jnp.where(first_half, pltpu.roll(t, LANES - ROT_HALF, 1), pltpu.roll(t, ROT_HALF, 1))
        return t * cos + partner * sin

    def proj(rows, lo, width):
        return jnp.dot(hbuf_ref[rows, :], w_ref[:, lo:lo + width], preferred_element_type=F32)

    glu = proj(slice(None), CA_OFF, 2 * CONV_WIDTH)
    u = glu[:, :CONV_WIDTH] * _sigmoid(glu[:, CONV_WIDTH:])
    for c in range(CONV_WIDTH // LANES):
        cl = slice(c * LANES, (c + 1) * LANES)
        ubuf_ref[c, 0:U_HALO, :] = jnp.where(is_first, 0.0, u[0:U_HALO, cl])
        ubuf_ref[c, U_HALO:U_HALO + tm, :] = u[U_HALO:U_HALO + tm, cl]
        ubuf_ref[c, U_HALO + tm:, :] = jnp.where(is_last, 0.0, u[U_HALO + tm:, cl])
    scg_ref[...] = _silu(proj(main, CG_OFF, CONV_WIDTH))

    for c2 in range(ATTN_WIDTH // (2 * LANES)):
        q = proj(main, Q_OFF + c2 * 2 * LANES, 2 * LANES)
        for c in range(2):
            dst = ACT_Q + (2 * c2 + c) * LANES
            act_ref[:, dst:dst + LANES] = (
                rope(q[:, c * LANES:(c + 1) * LANES]) * (HEAD_DIM ** -0.5 * LOG2E)).astype(BF16)
    kv = proj(main, K_OFF, 2 * KV_WIDTH)
    kT_ref[...] = rope(kv[:, :KV_WIDTH]).T.astype(BF16)
    v0, v1 = _dup_heads(kv[:, KV_WIDTH:])
    v_ref[:, :LANES] = v0.astype(BF16)
    v_ref[:, LANES:] = v1.astype(BF16)
    for c2 in range(ATTN_WIDTH // (2 * LANES)):
        lo = c2 * 2 * LANES
        act_ref[:, ACT_SAG + lo:ACT_SAG + lo + 2 * LANES] = _silu(proj(main, AG_OFF + lo, 2 * LANES)).astype(BF16)

    groups = CONV_ROWS // SUBLANES
    for r0 in range(0, tm, CONV_ROWS):
        slabs = []
        for cb in range(CONV_WIDTH // CONV_COLS):
            ch = slice(cb * CONV_COLS, (cb + 1) * CONV_COLS)
            acc = jnp.zeros((groups, SUBLANES, CONV_COLS), F32) + conv_b_ref[:, ch]
            for t in range(CONV_SIZE):
                top = r0 + U_HALO - CONV_HALF + t
                rows = jnp.concatenate(
                    [ubuf_ref[c, top:top + CONV_ROWS, :]
                     for c in range(cb * CONV_COLS // LANES, (cb + 1) * CONV_COLS // LANES)], axis=1)
                acc = acc + rows.reshape(groups, SUBLANES, CONV_COLS) * conv_w_ref[t, :, ch][None]
            slabs.append(acc.reshape(CONV_ROWS, CONV_COLS))
        cn = _silu(_norm_rows(jnp.concatenate(slabs, axis=1)) * cln_g_ref[...] + cln_b_ref[...])
        act_ref[r0:r0 + CONV_ROWS, ACT_C:ACT_C + CONV_WIDTH] = (
            cn * scg_ref[r0:r0 + CONV_ROWS, :]).astype(BF16)


def _in_proj(x, shift, scale, cos_t, sin_t, w_in, conv_w, conv_b, cln_g, cln_b):
    b, n, d = x.shape
    tm = TILE_ROWS
    hpt = tm // U_HALO
    n_h = n // U_HALO
    const2 = lambda bi, i: (0, 0)
    return pl.pallas_call(
        _in_proj_body,
        grid=(b, n // tm),
        in_specs=[
            pl.BlockSpec((None, tm, d), lambda bi, i: (bi, i, 0)),
            pl.BlockSpec((None, U_HALO, d), lambda bi, i: (bi, jnp.maximum(i * hpt - 1, 0), 0)),
            pl.BlockSpec((None, U_HALO, d), lambda bi, i: (bi, jnp.minimum((i + 1) * hpt, n_h - 1), 0)),
            pl.BlockSpec((None, 1, d), lambda bi, i: (bi, 0, 0)),
            pl.BlockSpec((None, 1, d), lambda bi, i: (bi, 0, 0)),
            pl.BlockSpec((tm, LANES), lambda bi, i: (i, 0)),
            pl.BlockSpec((tm, LANES), lambda bi, i: (i, 0)),
            pl.BlockSpec((d, IN_WIDTH), const2),
            pl.BlockSpec((CONV_SIZE, SUBLANES, CONV_WIDTH), lambda bi, i: (0, 0, 0)),
            pl.BlockSpec((1, CONV_WIDTH), const2),
            pl.BlockSpec((1, CONV_WIDTH), const2),
            pl.BlockSpec((1, CONV_WIDTH), const2),
        ],
        out_specs=[
            pl.BlockSpec((None, tm, ACT_WIDTH), lambda bi, i: (bi, i, 0)),
            pl.BlockSpec((None, KV_WIDTH, tm), lambda bi, i: (bi, 0, i)),
            pl.BlockSpec((None, tm, 2 * LANES), lambda bi, i: (bi, i, 0)),
        ],
        out_shape=[
            jax.ShapeDtypeStruct((b, n, ACT_WIDTH), BF16),
            jax.ShapeDtypeStruct((b, KV_WIDTH, n), BF16),
            jax.ShapeDtypeStruct((b, n, 2 * LANES), BF16),
        ],
        scratch_shapes=[
            pltpu.VMEM((tm + 2 * U_HALO, d), BF16),
            pltpu.VMEM((CONV_WIDTH // LANES, tm + 2 * U_HALO, LANES), F32),
            pltpu.VMEM((tm, CONV_WIDTH), F32),
        ],
        compiler_params=pltpu.CompilerParams(
            dimension_semantics=("arbitrary", "arbitrary"), vmem_limit_bytes=VMEM_LIMIT),
        name="in_proj",
    )(x, x, x, shift, scale, cos_t, sin_t, w_in, conv_w, conv_b, cln_g, cln_b)


def _mixer_body(sink_ref, act_ref,
                kT_ref, kT_prev_ref, kT_next_ref, v_ref, v_prev_ref, v_next_ref,
                kcT_ref, vc_ref, band_ref, w_out_ref,
                x_ref, gate_ref, pln_g_ref, pln_b_ref,
                o_ref,
                v4_ref, kcT4_ref, vc4_ref, mixed_ref, p_ref, esink_ref, kT4_ref, edge_ref):
    tq = x_ref.shape[0]
    i = pl.program_id(1)
    is_first = i == 0
    is_last = i == pl.num_programs(1) - 1
    n_qblk = tq // QBLK

    n_ctx_tiles = kcT_ref.shape[-1] // LANES
    n_loc_tiles = (QBLK + 2 * WINDOW) // LANES
    loc_cols = n_loc_tiles * LANES
    n_tiles = n_loc_tiles + n_ctx_tiles

    @pl.when(is_first)
    def _():
        for h in range(N_KV_HEADS):
            for g in range(GQA_GROUP):
                kcT4_ref[h, g * HEAD_DIM:(g + 1) * HEAD_DIM, :] = kcT_ref[h * HEAD_DIM:(h + 1) * HEAD_DIM, :]
            vc4_ref[h, :, 0:LANES] = vc_ref[:, h * LANES:(h + 1) * LANES]
            vc4_ref[h, :, LANES:] = jnp.ones((vc4_ref.shape[1], LANES), BF16)
            v4_ref[h, :, LANES:] = jnp.ones((v4_ref.shape[1], LANES), BF16)

    for h in range(N_KV_HEADS):
        lanes = slice(h * LANES, (h + 1) * LANES)
        v4_ref[h, 0:WINDOW, 0:LANES] = v_prev_ref[:, lanes]
        v4_ref[h, WINDOW:WINDOW + tq, 0:LANES] = v_ref[:, lanes]
        v4_ref[h, WINDOW + tq:, 0:LANES] = v_next_ref[:, lanes]

    def head_lanes(g):
        lane = lax.broadcasted_iota(jnp.int32, (1, GQA_GROUP * HEAD_DIM), 1)
        return (lane >= g * HEAD_DIM) & (lane < (g + 1) * HEAD_DIM)

    for h in range(N_KV_HEADS):
        rows = slice(h * HEAD_DIM, (h + 1) * HEAD_DIM)
        for g in range(GQA_GROUP):
            dst = slice(g * HEAD_DIM, (g + 1) * HEAD_DIM)
            kT4_ref[h, dst, 0:WINDOW] = kT_prev_ref[rows, :]
            kT4_ref[h, dst, WINDOW:WINDOW + tq] = kT_ref[rows, :]
            kT4_ref[h, dst, WINDOW + tq:] = kT_next_ref[rows, :]
    edge_ref[0] = band_ref[:, 0:WINDOW] + jnp.where(is_first, NEG_INF, 0.0)
    edge_ref[1] = band_ref[:, QBLK + WINDOW:] + jnp.where(is_last, NEG_INF, 0.0)

    def mask_lo(j, rr):
        if j == 0:
            return edge_ref[0, rr:rr + SM_ROWS, :]
        return band_ref[rr:rr + SM_ROWS, 0:WINDOW]

    def mask_hi(j, rr):
        if j == n_qblk - 1:
            return edge_ref[1, rr:rr + SM_ROWS, :]
        return band_ref[rr:rr + SM_ROWS, QBLK + WINDOW:]

    mixed_ref[:, ATTN_WIDTH:] = act_ref[:, ACT_C:ACT_C + CONV_WIDTH]

    for j in range(n_qblk):
        q0 = j * QBLK
        for h in range(N_KV_HEADS):
            hl = slice(h * GQA_GROUP * HEAD_DIM, (h + 1) * GQA_GROUP * HEAD_DIM)
            qb = act_ref[q0:q0 + QBLK, ACT_Q + hl.start:ACT_Q + hl.stop]
            lhs = jnp.concatenate(
                [jnp.where(head_lanes(g), qb, jnp.zeros_like(qb)) for g in range(GQA_GROUP)], axis=0)
            s_loc = jnp.dot(lhs, kT4_ref[h, :, q0:q0 + loc_cols], preferred_element_type=F32)
            s_ctx = jnp.dot(lhs, kcT4_ref[h], preferred_element_type=F32)
            for rc in range(0, GQA_GROUP * QBLK, SM_ROWS):
                g, rr = divmod(rc, QBLK)
                sink = sink_ref[h * GQA_GROUP + g] * LOG2E
                tiles = [s_loc[rc:rc + SM_ROWS, t * LANES:(t + 1) * LANES] for t in range(n_loc_tiles)]
                tiles += [s_ctx[rc:rc + SM_ROWS, t * LANES:(t + 1) * LANES] for t in range(n_ctx_tiles)]
                tiles[0] = tiles[0] + mask_lo(j, rr)
                tiles[n_loc_tiles - 1] = tiles[n_loc_tiles - 1] + mask_hi(j, rr)
                m = jnp.max(functools.reduce(jnp.maximum, tiles), axis=-1, keepdims=True)
                m = jnp.maximum(m, sink)
                esink_ref[h, rc:rc + SM_ROWS, :] = jnp.exp2(sink - m)
                for t in range(n_tiles):
                    p_ref[h, rc:rc + SM_ROWS, t * LANES:(t + 1) * LANES] = jnp.exp2(tiles[t] - m).astype(BF16)
            r = (jnp.dot(p_ref[h, :, 0:loc_cols], v4_ref[h, q0:q0 + loc_cols, :], preferred_element_type=F32)
                 + jnp.dot(p_ref[h, :, loc_cols:], vc4_ref[h], preferred_element_type=F32))
            out = r[:, 0:LANES] * (1.0 / (r[:, LANES:] + esink_ref[h]))
            first_of_pair = lax.broadcasted_iota(jnp.int32, (1, LANES), 1) < HEAD_DIM
            attn = jnp.concatenate(
                [jnp.where(first_of_pair, out[(2 * k) * QBLK:(2 * k + 1) * QBLK],
                           out[(2 * k + 1) * QBLK:(2 * k + 2) * QBLK]) for k in range(GQA_GROUP // 2)], axis=1)
            sag = act_ref[q0:q0 + QBLK, ACT_SAG + hl.start:ACT_SAG + hl.stop].astype(F32)
            mixed_ref[q0:q0 + QBLK, hl] = (attn * sag).astype(BF16)

    y = jnp.dot(mixed_ref[...], w_out_ref[...], preferred_element_type=F32)
    z = x_ref[...] + (gate_ref[...] * (1.0 / ALPHA)) * y
    o_ref[...] = _norm_rows(z, LN_EPS / ALPHA ** 2) * pln_g_ref[...] + pln_b_ref[...]


def _mixer(sink, act, kT, vv, kcT, vc, band, w_out, x, gate, pln_g, pln_b):
    b, n, d = x.shape
    tq = MIXER_ROWS
    m = kcT.shape[-1]
    qpt = tq // QBLK
    n_q = n // QBLK
    const2 = lambda bi, i: (0, 0)
    return pl.pallas_call(
        _mixer_body,
        grid=(b, n // tq),
        in_specs=[
            pl.BlockSpec(memory_space=pltpu.SMEM),
            pl.BlockSpec((None, tq, ACT_WIDTH), lambda bi, i: (bi, i, 0)),
            pl.BlockSpec((None, KV_WIDTH, tq), lambda bi, i: (bi, 0, i)),
            pl.BlockSpec((None, KV_WIDTH, WINDOW), lambda bi, i: (bi, 0, jnp.maximum(i * qpt - 1, 0))),
            pl.BlockSpec((None, KV_WIDTH, WINDOW), lambda bi, i: (bi, 0, jnp.minimum((i + 1) * qpt, n_q - 1))),
            pl.BlockSpec((None, tq, 2 * LANES), lambda bi, i: (bi, i, 0)),
            pl.BlockSpec((None, WINDOW, 2 * LANES), lambda bi, i: (bi, jnp.maximum(i * qpt - 1, 0), 0)),
            pl.BlockSpec((None, WINDOW, 2 * LANES), lambda bi, i: (bi, jnp.minimum((i + 1) * qpt, n_q - 1), 0)),
            pl.BlockSpec((None, KV_WIDTH, m), lambda bi, i: (bi, 0, 0)),
            pl.BlockSpec((None, m, 2 * LANES), lambda bi, i: (bi, 0, 0)),
            pl.BlockSpec((QBLK, QBLK + 2 * WINDOW), const2),
            pl.BlockSpec((d, d), const2),
            pl.BlockSpec((None, tq, d), lambda bi, i: (bi, i, 0)),
            pl.BlockSpec((None, 1, d), lambda bi, i: (bi, 0, 0)),
            pl.BlockSpec((1, d), const2),
            pl.BlockSpec((1, d), const2),
        ],
        out_specs=pl.BlockSpec((None, tq, d), lambda bi, i: (bi, i, 0)),
        out_shape=jax.ShapeDtypeStruct((b, n, d), F32),
        scratch_shapes=[
            pltpu.VMEM((N_KV_HEADS, tq + 2 * WINDOW, 2 * LANES), BF16),
            pltpu.VMEM((N_KV_HEADS, GQA_GROUP * HEAD_DIM, m), BF16),
            pltpu.VMEM((N_KV_HEADS, m, 2 * LANES), BF16),
            pltpu.VMEM((tq, d), BF16),
            pltpu.VMEM((N_KV_HEADS, GQA_GROUP * QBLK, QBLK + 2 * WINDOW + m), BF16),
            pltpu.VMEM((N_KV_HEADS, GQA_GROUP * QBLK, 1), F32),
            pltpu.VMEM((N_KV_HEADS, GQA_GROUP * HEAD_DIM, tq + 2 * WINDOW), BF16),
            pltpu.VMEM((2, QBLK, WINDOW), F32),
        ],
        compiler_params=pltpu.CompilerParams(
            dimension_semantics=("arbitrary", "arbitrary"), vmem_limit_bytes=VMEM_LIMIT),
        name="mixer",
    )(sink, act, kT, kT, kT, vv, vv, vv, kcT, vc, band, w_out, x, gate, pln_g, pln_b)


def _rope_tables(n):
    rows = n // GRID_W
    row = jnp.repeat(jnp.arange(rows, dtype=F32), GRID_W)
    colp = jnp.tile(jnp.arange(GRID_W, dtype=F32), rows)
    inv_freq = ROPE_BASE ** (-jnp.arange(0, ROT_AXIS_DIM, 2, dtype=F32) / ROT_AXIS_DIM)
    ang_row = row[:, None] * inv_freq
    ang_col = colp[:, None] * inv_freq
    cos_h = jnp.concatenate([jnp.cos(ang_row), jnp.cos(ang_row), jnp.cos(ang_col), jnp.cos(ang_col)], axis=-1)
    sin_h = jnp.concatenate([-jnp.sin(ang_row), jnp.sin(ang_row), -jnp.sin(ang_col), jnp.sin(ang_col)], axis=-1)
    reps = LANES // HEAD_DIM
    return jnp.tile(cos_h, (1, reps)), jnp.tile(sin_h, (1, reps))


def _band_bias():
    rel = jnp.arange(QBLK + 2 * WINDOW)[None, :] - jnp.arange(QBLK)[:, None]
    return jnp.where((rel >= 0) & (rel <= 2 * WINDOW), 0.0, NEG_INF).astype(F32)


def kernel(x, c, ctx, c_ctx, w_ada, b_ada, w_in, attn_sink, conv_w, conv_b, conv_ln_g, conv_ln_b,
           w_out, post_ln_g, post_ln_b):
    assert w_ada.shape[0] == DEPTH
    b, n, d = x.shape
    cond = jnp.zeros((ADA_ROWS, d), F32).at[:b].set(c).at[b].set(c_ctx)
    mods = _adaln(cond, w_ada[0], b_ada[0][None, :])
    shift, scale, gate = (mods[:b, k * d:(k + 1) * d][:, None, :] for k in range(3))
    shift_c, scale_c = (mods[b:b + 1, k * d:(k + 1) * d] for k in range(2))

    w_in_b = w_in[0].astype(BF16)
    kcT, vc = _ctx_kv(ctx, shift_c, scale_c, w_in_b[:, K_OFF:AG_OFF])
    cos_t, sin_t = _rope_tables(n)
    conv_w_rep = jnp.broadcast_to(conv_w[0][:, None, :], (CONV_SIZE, SUBLANES, CONV_WIDTH))
    act, kT, vv = _in_proj(x, shift, scale, cos_t, sin_t, w_in_b,
                           conv_w_rep, conv_b[0][None, :], conv_ln_g[0][None, :], conv_ln_b[0][None, :])
    return _mixer(attn_sink[0], act, kT, vv, kcT, vc, _band_bias(),
                  w_out[0].astype(BF16), x, gate, post_ln_g[0][None, :], post_ln_b[0][None, :])
```

```python
import functools
import math

import jax
import jax.numpy as jnp
from jax import lax
from jax.experimental import pallas as pl
from jax.experimental.pallas import tpu as pltpu

F32 = jnp.float32
BF16 = jnp.bfloat16

D_MODEL = 1024
GRID_W = 64
HEAD_DIM = 64
N_Q_HEADS = 8
N_KV_HEADS = 2
GQA_GROUP = N_Q_HEADS // N_KV_HEADS
ATTN_WIDTH = N_Q_HEADS * HEAD_DIM
KV_WIDTH = N_KV_HEADS * HEAD_DIM
CONV_WIDTH = D_MODEL - ATTN_WIDTH
CONV_SIZE = 31
CONV_HALF = CONV_SIZE // 2
WINDOW = 128
QBLK = 128
ROPE_BASE = 10000.0
ROT_AXIS_DIM = HEAD_DIM // 2
ROT_HALF = ROT_AXIS_DIM // 2
LN_EPS = 1e-6
NEG_INF = -1e30
DEPTH = 1
ALPHA = (2.0 * DEPTH) ** 0.25

Q_OFF = 0
K_OFF = Q_OFF + ATTN_WIDTH
V_OFF = K_OFF + KV_WIDTH
AG_OFF = V_OFF + KV_WIDTH
CA_OFF = AG_OFF + ATTN_WIDTH
CB_OFF = CA_OFF + CONV_WIDTH
CG_OFF = CB_OFF + CONV_WIDTH
IN_WIDTH = CG_OFF + CONV_WIDTH

ACT_Q = 0
ACT_C = ATTN_WIDTH
ACT_SAG = ACT_C + CONV_WIDTH
ACT_WIDTH = ACT_SAG + ATTN_WIDTH

LANES = 128
SUBLANES = 8
CONV_ROWS = 32
CONV_COLS = 256
SM_ROWS = 32
LOG2E = math.log2(math.e)
U_HALO = 16
TILE_ROWS = 1024
MIXER_ROWS = 1024
ADA_ROWS = 40
ADA_COLS = 1024
CTX_BATCH = 8
VMEM_LIMIT = 56 * 1024 * 1024


def _norm_rows(x, eps=LN_EPS):
    mu = jnp.mean(x, axis=-1, keepdims=True)
    xc = x - mu
    var = jnp.mean(xc * xc, axis=-1, keepdims=True)
    return xc * lax.rsqrt(var + eps)


def _sigmoid(x):
    return 0.5 + 0.5 * jnp.tanh(0.5 * x)


def _silu(x):
    hx = 0.5 * x
    return hx + hx * jnp.tanh(hx)


def _adaln_body(c_ref, w_ref, b_ref, o_ref):
    o_ref[...] = jnp.dot(_silu(c_ref[...]), w_ref[...], preferred_element_type=F32) + b_ref[...]


def _adaln(cond, w_ada, b_ada):
    n_cols = w_ada.shape[1]
    return pl.pallas_call(
        _adaln_body,
        grid=(n_cols // ADA_COLS,),
        in_specs=[
            pl.BlockSpec((ADA_ROWS, D_MODEL), lambda j: (0, 0)),
            pl.BlockSpec((D_MODEL, ADA_COLS), lambda j: (0, j)),
            pl.BlockSpec((1, ADA_COLS), lambda j: (0, j)),
        ],
        out_specs=pl.BlockSpec((ADA_ROWS, ADA_COLS), lambda j: (0, j)),
        out_shape=jax.ShapeDtypeStruct((ADA_ROWS, n_cols), F32),
        compiler_params=pltpu.CompilerParams(dimension_semantics=("arbitrary",)),
        name="adaln",
    )(cond, w_ada, b_ada)


def _dup_heads(v):
    lane = lax.broadcasted_iota(jnp.int32, (1, LANES), 1)
    swapped = pltpu.roll(v, HEAD_DIM, 1)
    low = lane < HEAD_DIM
    return jnp.where(low, v, swapped), jnp.where(low, swapped, v)


def _ctx_kv_body(ctx_ref, shift_ref, scale_ref, w_ref, kT_ref, v_ref):
    nb, m, d = ctx_ref.shape
    h = _norm_rows(ctx_ref[...].reshape(nb * m, d)) * (1.0 + scale_ref[...]) + shift_ref[...]
    kv = jnp.dot(h.astype(BF16), w_ref[...], preferred_element_type=F32)
    for bi in range(nb):
        rows = slice(bi * m, (bi + 1) * m)
        kT_ref[bi] = kv[rows, :KV_WIDTH].T.astype(BF16)
        v0, v1 = _dup_heads(kv[rows, KV_WIDTH:])
        v_ref[bi, :, :LANES] = v0.astype(BF16)
        v_ref[bi, :, LANES:] = v1.astype(BF16)


def _ctx_kv(ctx, shift_c, scale_c, w_kv):
    b, m, d = ctx.shape
    nb = CTX_BATCH
    return pl.pallas_call(
        _ctx_kv_body,
        grid=(b // nb,),
        in_specs=[
            pl.BlockSpec((nb, m, d), lambda i: (i, 0, 0)),
            pl.BlockSpec((1, d), lambda i: (0, 0)),
            pl.BlockSpec((1, d), lambda i: (0, 0)),
            pl.BlockSpec((d, 2 * KV_WIDTH), lambda i: (0, 0)),
        ],
        out_specs=[
            pl.BlockSpec((nb, KV_WIDTH, m), lambda i: (i, 0, 0)),
            pl.BlockSpec((nb, m, 2 * LANES), lambda i: (i, 0, 0)),
        ],
        out_shape=[
            jax.ShapeDtypeStruct((b, KV_WIDTH, m), BF16),
            jax.ShapeDtypeStruct((b, m, 2 * LANES), BF16),
        ],
        compiler_params=pltpu.CompilerParams(dimension_semantics=("arbitrary",)),
        name="ctx_kv",
    )(ctx, shift_c, scale_c, w_kv)


def _in_proj_body(x_ref, x_prev_ref, x_next_ref, shift_ref, scale_ref, cos_ref, sin_ref, w_ref,
                  conv_w_ref, conv_b_ref, cln_g_ref, cln_b_ref,
                  act_ref, kT_ref, v_ref,
                  hbuf_ref, ubuf_ref, scg_ref):
    tm = x_ref.shape[0]
    i = pl.program_id(1)
    is_first = i == 0
    is_last = i == pl.num_programs(1) - 1

    def modulate(xv):
        return (_norm_rows(xv) * (1.0 + scale_ref[...]) + shift_ref[...]).astype(BF16)

    hbuf_ref[0:U_HALO, :] = modulate(x_prev_ref[...])
    hbuf_ref[U_HALO:U_HALO + tm, :] = modulate(x_ref[...])
    hbuf_ref[U_HALO + tm:, :] = modulate(x_next_ref[...])
    main = slice(U_HALO, U_HALO + tm)

    cos = cos_ref[...]
    sin = sin_ref[...]
    lane = lax.broadcasted_iota(jnp.int32, (1, LANES), 1)
    first_half = (lane % ROT_AXIS_DIM) < ROT_HALF

    def rope(t):
        partner = jnp.where(first_half, pltpu.roll(t, LANES - ROT_HALF, 1), pltpu.roll(t, ROT_HALF, 1))
        return t * cos + partner * sin

    def proj(rows, lo, width):
        return jnp.dot(hbuf_ref[rows, :], w_ref[:, lo:lo + width], preferred_element_type=F32)

    glu = proj(slice(None), CA_OFF, 2 * CONV_WIDTH)
    u = glu[:, :CONV_WIDTH] * _sigmoid(glu[:, CONV_WIDTH:])
    for c in range(CONV_WIDTH // LANES):
        cl = slice(c * LANES, (c + 1) * LANES)
        ubuf_ref[c, 0:U_HALO, :] = jnp.where(is_first, 0.0, u[0:U_HALO, cl])
        ubuf_ref[c, U_HALO:U_HALO + tm, :] = u[U_HALO:U_HALO + tm, cl]
        ubuf_ref[c, U_HALO + tm:, :] = jnp.where(is_last, 0.0, u[U_HALO + tm:, cl])
    scg_ref[...] = _silu(proj(main, CG_OFF, CONV_WIDTH))

    for c2 in range(ATTN_WIDTH // (2 * LANES)):
        q = proj(main, Q_OFF + c2 * 2 * LANES, 2 * LANES)
        for c in range(2):
            dst = ACT_Q + (2 * c2 + c) * LANES
            act_ref[:, dst:dst + LANES] = (
                rope(q[:, c * LANES:(c + 1) * LANES]) * (HEAD_DIM ** -0.5 * LOG2E)).astype(BF16)
    kv = proj(main, K_OFF, 2 * KV_WIDTH)
    kT_ref[...] = rope(kv[:, :KV_WIDTH]).T.astype(BF16)
    v0, v1 = _dup_heads(kv[:, KV_WIDTH:])
    v_ref[:, :LANES] = v0.astype(BF16)
    v_ref[:, LANES:] = v1.astype(BF16)
    for c2 in range(ATTN_WIDTH // (2 * LANES)):
        lo = c2 * 2 * LANES
        act_ref[:, ACT_SAG + lo:ACT_SAG + lo + 2 * LANES] = _silu(proj(main, AG_OFF + lo, 2 * LANES)).astype(BF16)

    groups = CONV_ROWS // SUBLANES
    for r0 in range(0, tm, CONV_ROWS):
        slabs = []
        for cb in range(CONV_WIDTH // CONV_COLS):
            ch = slice(cb * CONV_COLS, (cb + 1) * CONV_COLS)
            acc = jnp.zeros((groups, SUBLANES, CONV_COLS), F32) + conv_b_ref[:, ch]
            for t in range(CONV_SIZE):
                top = r0 + U_HALO - CONV_HALF + t
                rows = jnp.concatenate(
                    [ubuf_ref[c, top:top + CONV_ROWS, :]
                     for c in range(cb * CONV_COLS // LANES, (cb + 1) * CONV_COLS // LANES)], axis=1)
                acc = acc + rows.reshape(groups, SUBLANES, CONV_COLS) * conv_w_ref[t, :, ch][None]
            slabs.append(acc.reshape(CONV_ROWS, CONV_COLS))
        cn = _silu(_norm_rows(jnp.concatenate(slabs, axis=1)) * cln_g_ref[...] + cln_b_ref[...])
        act_ref[r0:r0 + CONV_ROWS, ACT_C:ACT_C + CONV_WIDTH] = (
            cn * scg_ref[r0:r0 + CONV_ROWS, :]).astype(BF16)


def _in_proj(x, shift, scale, cos_t, sin_t, w_in, conv_w, conv_b, cln_g, cln_b):
    b, n, d = x.shape
    tm = TILE_ROWS
    hpt = tm // U_HALO
    n_h = n // U_HALO
    const2 = lambda bi, i: (0, 0)
    return pl.pallas_call(
        _in_proj_body,
        grid=(b, n // tm),
        in_specs=[
            pl.BlockSpec((None, tm, d), lambda bi, i: (bi, i, 0)),
            pl.BlockSpec((None, U_HALO, d), lambda bi, i: (bi, jnp.maximum(i * hpt - 1, 0), 0)),
            pl.BlockSpec((None, U_HALO, d), lambda bi, i: (bi, jnp.minimum((i + 1) * hpt, n_h - 1), 0)),
            pl.BlockSpec((None, 1, d), lambda bi, i: (bi, 0, 0)),
            pl.BlockSpec((None, 1, d), lambda bi, i: (bi, 0, 0)),
            pl.BlockSpec((tm, LANES), lambda bi, i: (i, 0)),
            pl.BlockSpec((tm, LANES), lambda bi, i: (i, 0)),
            pl.BlockSpec((d, IN_WIDTH), const2),
            pl.BlockSpec((CONV_SIZE, SUBLANES, CONV_WIDTH), lambda bi, i: (0, 0, 0)),
            pl.BlockSpec((1, CONV_WIDTH), const2),
            pl.BlockSpec((1, CONV_WIDTH), const2),
            pl.BlockSpec((1, CONV_WIDTH), const2),
        ],
        out_specs=[
            pl.BlockSpec((None, tm, ACT_WIDTH), lambda bi, i: (bi, i, 0)),
            pl.BlockSpec((None, KV_WIDTH, tm), lambda bi, i: (bi, 0, i)),
            pl.BlockSpec((None, tm, 2 * LANES), lambda bi, i: (bi, i, 0)),
        ],
        out_shape=[
            jax.ShapeDtypeStruct((b, n, ACT_WIDTH), BF16),
            jax.ShapeDtypeStruct((b, KV_WIDTH, n), BF16),
            jax.ShapeDtypeStruct((b, n, 2 * LANES), BF16),
        ],
        scratch_shapes=[
            pltpu.VMEM((tm + 2 * U_HALO, d), BF16),
            pltpu.VMEM((CONV_WIDTH // LANES, tm + 2 * U_HALO, LANES), F32),
            pltpu.VMEM((tm, CONV_WIDTH), F32),
        ],
        compiler_params=pltpu.CompilerParams(
            dimension_semantics=("arbitrary", "arbitrary"), vmem_limit_bytes=VMEM_LIMIT),
        name="in_proj",
    )(x, x, x, shift, scale, cos_t, sin_t, w_in, conv_w, conv_b, cln_g, cln_b)


def _mixer_body(sink_ref, act_ref,
                kT_ref, kT_prev_ref, kT_next_ref, v_ref, v_prev_ref, v_next_ref,
                kcT_ref, vc_ref, band_ref, w_out_ref,
                x_ref, gate_ref, pln_g_ref, pln_b_ref,
                o_ref,
                v4_ref, kcT4_ref, vc4_ref, mixed_ref, p_ref, esink_ref, kT4_ref, edge_ref):
    tq = x_ref.shape[0]
    i = pl.program_id(1)
    is_first = i == 0
    is_last = i == pl.num_programs(1) - 1
    n_qblk = tq // QBLK

    n_ctx_tiles = kcT_ref.shape[-1] // LANES
    n_loc_tiles = (QBLK + 2 * WINDOW) // LANES
    loc_cols = n_loc_tiles * LANES
    n_tiles = n_loc_tiles + n_ctx_tiles

    @pl.when(is_first)
    def _():
        for h in range(N_KV_HEADS):
            for g in range(GQA_GROUP):
                kcT4_ref[h, g * HEAD_DIM:(g + 1) * HEAD_DIM, :] = kcT_ref[h * HEAD_DIM:(h + 1) * HEAD_DIM, :]
            vc4_ref[h, :, 0:LANES] = vc_ref[:, h * LANES:(h + 1) * LANES]
            vc4_ref[h, :, LANES:] = jnp.ones((vc4_ref.shape[1], LANES), BF16)
            v4_ref[h, :, LANES:] = jnp.ones((v4_ref.shape[1], LANES), BF16)

    for h in range(N_KV_HEADS):
        lanes = slice(h * LANES, (h + 1) * LANES)
        v4_ref[h, 0:WINDOW, 0:LANES] = v_prev_ref[:, lanes]
        v4_ref[h, WINDOW:WINDOW + tq, 0:LANES] = v_ref[:, lanes]
        v4_ref[h, WINDOW + tq:, 0:LANES] = v_next_ref[:, lanes]

    def head_lanes(g):
        lane = lax.broadcasted_iota(jnp.int32, (1, GQA_GROUP * HEAD_DIM), 1)
        return (lane >= g * HEAD_DIM) & (lane < (g + 1) * HEAD_DIM)

    for h in range(N_KV_HEADS):
        rows = slice(h * HEAD_DIM, (h + 1) * HEAD_DIM)
        for g in range(GQA_GROUP):
            dst = slice(g * HEAD_DIM, (g + 1) * HEAD_DIM)
            kT4_ref[h, dst, 0:WINDOW] = kT_prev_ref[rows, :]
            kT4_ref[h, dst, WINDOW:WINDOW + tq] = kT_ref[rows, :]
            kT4_ref[h, dst, WINDOW + tq:] = kT_next_ref[rows, :]
    edge_ref[0] = band_ref[:, 0:WINDOW] + jnp.where(is_first, NEG_INF, 0.0)
    edge_ref[1] = band_ref[:, QBLK + WINDOW:] + jnp.where(is_last, NEG_INF, 0.0)

    def mask_lo(j, rr):
        if j == 0:
            return edge_ref[0, rr:rr + SM_ROWS, :]
        return band_ref[rr:rr + SM_ROWS, 0:WINDOW]

    def mask_hi(j, rr):
        if j == n_qblk - 1:
            return edge_ref[1, rr:rr + SM_ROWS, :]
        return band_ref[rr:rr + SM_ROWS, QBLK + WINDOW:]

    mixed_ref[:, ATTN_WIDTH:] = act_ref[:, ACT_C:ACT_C + CONV_WIDTH]

    for j in range(n_qblk):
        q0 = j * QBLK
        for h in range(N_KV_HEADS):
            hl = slice(h * GQA_GROUP * HEAD_DIM, (h + 1) * GQA_GROUP * HEAD_DIM)
            qb = act_ref[q0:q0 + QBLK, ACT_Q + hl.start:ACT_Q + hl.stop]
            lhs = jnp.concatenate(
                [jnp.where(head_lanes(g), qb, jnp.zeros_like(qb)) for g in range(GQA_GROUP)], axis=0)
            s_loc = jnp.dot(lhs, kT4_ref[h, :, q0:q0 + loc_cols], preferred_element_type=F32)
            s_ctx = jnp.dot(lhs, kcT4_ref[h], preferred_element_type=F32)
            for rc in range(0, GQA_GROUP * QBLK, SM_ROWS):
                g, rr = divmod(rc, QBLK)
                sink = sink_ref[h * GQA_GROUP + g] * LOG2E
                tiles = [s_loc[rc:rc + SM_ROWS, t * LANES:(t + 1) * LANES] for t in range(n_loc_tiles)]
                tiles += [s_ctx[rc:rc + SM_ROWS, t * LANES:(t + 1) * LANES] for t in range(n_ctx_tiles)]
                tiles[0] = tiles[0] + mask_lo(j, rr)
                tiles[n_loc_tiles - 1] = tiles[n_loc_tiles - 1] + mask_hi(j, rr)
                m = jnp.max(functools.reduce(jnp.maximum, tiles), axis=-1, keepdims=True)
                m = jnp.maximum(m, sink)
                esink_ref[h, rc:rc + SM_ROWS, :] = jnp.exp2(sink - m)
                for t in range(n_tiles):
                    p_ref[h, rc:rc + SM_ROWS, t * LANES:(t + 1) * LANES] = jnp.exp2(tiles[t] - m).astype(BF16)
            r = (jnp.dot(p_ref[h, :, 0:loc_cols], v4_ref[h, q0:q0 + loc_cols, :], preferred_element_type=F32)
                 + jnp.dot(p_ref[h, :, loc_cols:], vc4_ref[h], preferred_element_type=F32))
            out = r[:, 0:LANES] * (1.0 / (r[:, LANES:] + esink_ref[h]))
            first_of_pair = lax.broadcasted_iota(jnp.int32, (1, LANES), 1) < HEAD_DIM
            attn = jnp.concatenate(
                [jnp.where(first_of_pair, out[(2 * k) * QBLK:(2 * k + 1) * QBLK],
                           out[(2 * k + 1) * QBLK:(2 * k + 2) * QBLK]) for k in range(GQA_GROUP // 2)], axis=1)
            sag = act_ref[q0:q0 + QBLK, ACT_SAG + hl.start:ACT_SAG + hl.stop].astype(F32)
            mixed_ref[q0:q0 + QBLK, hl] = (attn * sag).astype(BF16)

    y = jnp.dot(mixed_ref[...], w_out_ref[...], preferred_element_type=F32)
    z = x_ref[...] + (gate_ref[...] * (1.0 / ALPHA)) * y
    o_ref[...] = _norm_rows(z, LN_EPS / ALPHA ** 2) * pln_g_ref[...] + pln_b_ref[...]


def _mixer(sink, act, kT, vv, kcT, vc, band, w_out, x, gate, pln_g, pln_b):
    b, n, d = x.shape
    tq = MIXER_ROWS
    m = kcT.shape[-1]
    qpt = tq // QBLK
    n_q = n // QBLK
    const2 = lambda bi, i: (0, 0)
    return pl.pallas_call(
        _mixer_body,
        grid=(b, n // tq),
        in_specs=[
            pl.BlockSpec(memory_space=pltpu.SMEM),
            pl.BlockSpec((None, tq, ACT_WIDTH), lambda bi, i: (bi, i, 0)),
            pl.BlockSpec((None, KV_WIDTH, tq), lambda bi, i: (bi, 0, i)),
            pl.BlockSpec((None, KV_WIDTH, WINDOW), lambda bi, i: (bi, 0, jnp.maximum(i * qpt - 1, 0))),
            pl.BlockSpec((None, KV_WIDTH, WINDOW), lambda bi, i: (bi, 0, jnp.minimum((i + 1) * qpt, n_q - 1))),
            pl.BlockSpec((None, tq, 2 * LANES), lambda bi, i: (bi, i, 0)),
            pl.BlockSpec((None, WINDOW, 2 * LANES), lambda bi, i: (bi, jnp.maximum(i * qpt - 1, 0), 0)),
            pl.BlockSpec((None, WINDOW, 2 * LANES), lambda bi, i: (bi, jnp.minimum((i + 1) * qpt, n_q - 1), 0)),
            pl.BlockSpec((None, KV_WIDTH, m), lambda bi, i: (bi, 0, 0)),
            pl.BlockSpec((None, m, 2 * LANES), lambda bi, i: (bi, 0, 0)),
            pl.BlockSpec((QBLK, QBLK + 2 * WINDOW), const2),
            pl.BlockSpec((d, d), const2),
            pl.BlockSpec((None, tq, d), lambda bi, i: (bi, i, 0)),
            pl.BlockSpec((None, 1, d), lambda bi, i: (bi, 0, 0)),
            pl.BlockSpec((1, d), const2),
            pl.BlockSpec((1, d), const2),
        ],
        out_specs=pl.BlockSpec((None, tq, d), lambda bi, i: (bi, i, 0)),
        out_shape=jax.ShapeDtypeStruct((b, n, d), F32),
        scratch_shapes=[
            pltpu.VMEM((N_KV_HEADS, tq + 2 * WINDOW, 2 * LANES), BF16),
            pltpu.VMEM((N_KV_HEADS, GQA_GROUP * HEAD_DIM, m), BF16),
            pltpu.VMEM((N_KV_HEADS, m, 2 * LANES), BF16),
            pltpu.VMEM((tq, d), BF16),
            pltpu.VMEM((N_KV_HEADS, GQA_GROUP * QBLK, QBLK + 2 * WINDOW + m), BF16),
            pltpu.VMEM((N_KV_HEADS, GQA_GROUP * QBLK, 1), F32),
            pltpu.VMEM((N_KV_HEADS, GQA_GROUP * HEAD_DIM, tq + 2 * WINDOW), BF16),
            pltpu.VMEM((2, QBLK, WINDOW), F32),
        ],
        compiler_params=pltpu.CompilerParams(
            dimension_semantics=("arbitrary", "arbitrary"), vmem_limit_bytes=VMEM_LIMIT),
        name="mixer",
    )(sink, act, kT, kT, kT, vv, vv, vv, kcT, vc, band, w_out, x, gate, pln_g, pln_b)


def _rope_tables(n):
    rows = n // GRID_W
    row = jnp.repeat(jnp.arange(rows, dtype=F32), GRID_W)
    colp = jnp.tile(jnp.arange(GRID_W, dtype=F32), rows)
    inv_freq = ROPE_BASE ** (-jnp.arange(0, ROT_AXIS_DIM, 2, dtype=F32) / ROT_AXIS_DIM)
    ang_row = row[:, None] * inv_freq
    ang_col = colp[:, None] * inv_freq
    cos_h = jnp.concatenate([jnp.cos(ang_row), jnp.cos(ang_row), jnp.cos(ang_col), jnp.cos(ang_col)], axis=-1)
    sin_h = jnp.concatenate([-jnp.sin(ang_row), jnp.sin(ang_row), -jnp.sin(ang_col), jnp.sin(ang_col)], axis=-1)
    reps = LANES // HEAD_DIM
    return jnp.tile(cos_h, (1, reps)), jnp.tile(sin_h, (1, reps))


def _band_bias():
    rel = jnp.arange(QBLK + 2 * WINDOW)[None, :] - jnp.arange(QBLK)[:, None]
    return jnp.where((rel >= 0) & (rel <= 2 * WINDOW), 0.0, NEG_INF).astype(F32)


def kernel(x, c, ctx, c_ctx, w_ada, b_ada, w_in, attn_sink, conv_w, conv_b, conv_ln_g, conv_ln_b,
           w_out, post_ln_g, post_ln_b):
    assert w_ada.shape[0] == DEPTH
    b, n, d = x.shape
    cond = jnp.zeros((ADA_ROWS, d), F32).at[:b].set(c).at[b].set(c_ctx)
    mods = _adaln(cond, w_ada[0], b_ada[0][None, :])
    shift, scale, gate = (mods[:b, k * d:(k + 1) * d][:, None, :] for k in range(3))
    shift_c, scale_c = (mods[b:b + 1, k * d:(k + 1) * d] for k in range(2))

    w_in_b = w_in[0].astype(BF16)
    kcT, vc = _ctx_kv(ctx, shift_c, scale_c, w_in_b[:, K_OFF:AG_OFF])
    cos_t, sin_t = _rope_tables(n)
    conv_w_rep = jnp.broadcast_to(conv_w[0][:, None, :], (CONV_SIZE, SUBLANES, CONV_WIDTH))
    act, kT, vv = _in_proj(x, shift, scale, cos_t, sin_t, w_in_b,
                           conv_w_rep, conv_b[0][None, :], conv_ln_g[0][None, :], conv_ln_b[0][None, :])
    return _mixer(attn_sink[0], act, kT, vv, kcT, vc, _band_bias(),
                  w_out[0].astype(BF16), x, gate, post_ln_g[0][None, :], post_ln_b[0][None, :])
```

```python
import functools
import math

import jax
import jax.numpy as jnp
from jax import lax
from jax.experimental import pallas as pl
from jax.experimental.pallas import tpu as pltpu

F32 = jnp.float32
BF16 = jnp.bfloat16

D_MODEL = 1024
GRID_W = 64
HEAD_DIM = 64
N_Q_HEADS = 8
N_KV_HEADS = 2
GQA_GROUP = N_Q_HEADS // N_KV_HEADS
ATTN_WIDTH = N_Q_HEADS * HEAD_DIM
KV_WIDTH = N_KV_HEADS * HEAD_DIM
CONV_WIDTH = D_MODEL - ATTN_WIDTH
CONV_SIZE = 31
CONV_HALF = CONV_SIZE // 2
WINDOW = 128
QBLK = 128
ROPE_BASE = 10000.0
ROT_AXIS_DIM = HEAD_DIM // 2
ROT_HALF = ROT_AXIS_DIM // 2
LN_EPS = 1e-6
NEG_INF = -1e30
DEPTH = 1
ALPHA = (2.0 * DEPTH) ** 0.25

Q_OFF = 0
K_OFF = Q_OFF + ATTN_WIDTH
V_OFF = K_OFF + KV_WIDTH
AG_OFF = V_OFF + KV_WIDTH
CA_OFF = AG_OFF + ATTN_WIDTH
CB_OFF = CA_OFF + CONV_WIDTH
CG_OFF = CB_OFF + CONV_WIDTH
IN_WIDTH = CG_OFF + CONV_WIDTH

ACT_Q = 0
ACT_C = ATTN_WIDTH
ACT_SAG = ACT_C + CONV_WIDTH
ACT_WIDTH = ACT_SAG + ATTN_WIDTH

LANES = 128
SUBLANES = 8
CONV_ROWS = 32
CONV_COLS = 256
SM_ROWS = 32
LOG2E = math.log2(math.e)
U_HALO = 16
TILE_ROWS = 1024
MIXER_ROWS = 1024
ADA_ROWS = 40
ADA_COLS = 1024
CTX_BATCH = 8
VMEM_LIMIT = 56 * 1024 * 1024


def _norm_rows(x, eps=LN_EPS):
    mu = jnp.mean(x, axis=-1, keepdims=True)
    xc = x - mu
    var = jnp.mean(xc * xc, axis=-1, keepdims=True)
    return xc * lax.rsqrt(var + eps)


def _sigmoid(x):
    return 0.5 + 0.5 * jnp.tanh(0.5 * x)


def _silu(x):
    hx = 0.5 * x
    return hx + hx * jnp.tanh(hx)


def _adaln_body(c_ref, w_ref, b_ref, o_ref):
    o_ref[...] = jnp.dot(_silu(c_ref[...]), w_ref[...], preferred_element_type=F32) + b_ref[...]


def _adaln(cond, w_ada, b_ada):
    n_cols = w_ada.shape[1]
    return pl.pallas_call(
        _adaln_body,
        grid=(n_cols // ADA_COLS,),
        in_specs=[
            pl.BlockSpec((ADA_ROWS, D_MODEL), lambda j: (0, 0)),
            pl.BlockSpec((D_MODEL, ADA_COLS), lambda j: (0, j)),
            pl.BlockSpec((1, ADA_COLS), lambda j: (0, j)),
        ],
        out_specs=pl.BlockSpec((ADA_ROWS, ADA_COLS), lambda j: (0, j)),
        out_shape=jax.ShapeDtypeStruct((ADA_ROWS, n_cols), F32),
        compiler_params=pltpu.CompilerParams(dimension_semantics=("arbitrary",)),
        name="adaln",
    )(cond, w_ada, b_ada)


def _dup_heads(v):
    lane = lax.broadcasted_iota(jnp.int32, (1, LANES), 1)
    swapped = pltpu.roll(v, HEAD_DIM, 1)
    low = lane < HEAD_DIM
    return jnp.where(low, v, swapped), jnp.where(low, swapped, v)


def _ctx_kv_body(ctx_ref, shift_ref, scale_ref, w_ref, kT_ref, v_ref):
    nb, m, d = ctx_ref.shape
    h = _norm_rows(ctx_ref[...].reshape(nb * m, d)) * (1.0 + scale_ref[...]) + shift_ref[...]
    kv = jnp.dot(h.astype(BF16), w_ref[...], preferred_element_type=F32)
    for bi in range(nb):
        rows = slice(bi * m, (bi + 1) * m)
        kT_ref[bi] = kv[rows, :KV_WIDTH].T.astype(BF16)
        v0, v1 = _dup_heads(kv[rows, KV_WIDTH:])
        v_ref[bi, :, :LANES] = v0.astype(BF16)
        v_ref[bi, :, LANES:] = v1.astype(BF16)


def _ctx_kv(ctx, shift_c, scale_c, w_kv):
    b, m, d = ctx.shape
    nb = CTX_BATCH
    return pl.pallas_call(
        _ctx_kv_body,
        grid=(b // nb,),
        in_specs=[
            pl.BlockSpec((nb, m, d), lambda i: (i, 0, 0)),
            pl.BlockSpec((1, d), lambda i: (0, 0)),
            pl.BlockSpec((1, d), lambda i: (0, 0)),
            pl.BlockSpec((d, 2 * KV_WIDTH), lambda i: (0, 0)),
        ],
        out_specs=[
            pl.BlockSpec((nb, KV_WIDTH, m), lambda i: (i, 0, 0)),
            pl.BlockSpec((nb, m, 2 * LANES), lambda i: (i, 0, 0)),
        ],
        out_shape=[
            jax.ShapeDtypeStruct((b, KV_WIDTH, m), BF16),
            jax.ShapeDtypeStruct((b, m, 2 * LANES), BF16),
        ],
        compiler_params=pltpu.CompilerParams(dimension_semantics=("arbitrary",)),
        name="ctx_kv",
    )(ctx, shift_c, scale_c, w_kv)


def _in_proj_body(x_ref, x_prev_ref, x_next_ref, shift_ref, scale_ref, cos_ref, sin_ref, w_ref,
                  conv_w_ref, conv_b_ref, cln_g_ref, cln_b_ref,
                  act_ref, kT_ref, v_ref,
                  hbuf_ref, ubuf_ref, scg_ref):
    tm = x_ref.shape[0]
    i = pl.program_id(1)
    is_first = i == 0
    is_last = i == pl.num_programs(1) - 1

    def modulate(xv):
        return (_norm_rows(xv) * (1.0 + scale_ref[...]) + shift_ref[...]).astype(BF16)

    hbuf_ref[0:U_HALO, :] = modulate(x_prev_ref[...])
    hbuf_ref[U_HALO:U_HALO + tm, :] = modulate(x_ref[...])
    hbuf_ref[U_HALO + tm:, :] = modulate(x_next_ref[...])
    main = slice(U_HALO, U_HALO + tm)

    cos = cos_ref[...]
    sin = sin_ref[...]
    lane = lax.broadcasted_iota(jnp.int32, (1, LANES), 1)
    first_half = (lane % ROT_AXIS_DIM) < ROT_HALF

    def rope(t):
        partner = jnp.where(first_half, pltpu.roll(t, LANES - ROT_HALF, 1), pltpu.roll(t, ROT_HALF, 1))
        return t * cos + partner * sin

    def proj(rows, lo, width):
        return jnp.dot(hbuf_ref[rows, :], w_ref[:, lo:lo + width], preferred_element_type=F32)

    glu = proj(slice(None), CA_OFF, 2 * CONV_WIDTH)
    u = glu[:, :CONV_WIDTH] * _sigmoid(glu[:, CONV_WIDTH:])
    for c in range(CONV_WIDTH // LANES):
        cl = slice(c * LANES, (c + 1) * LANES)
        ubuf_ref[c, 0:U_HALO, :] = jnp.where(is_first, 0.0, u[0:U_HALO, cl])
        ubuf_ref[c, U_HALO:U_HALO + tm, :] = u[U_HALO:U_HALO + tm, cl]
        ubuf_ref[c, U_HALO + tm:, :] = jnp.where(is_last, 0.0, u[U_HALO + tm:, cl])
    scg_ref[...] = _silu(proj(main, CG_OFF, CONV_WIDTH))

    for c2 in range(ATTN_WIDTH // (2 * LANES)):
        q = proj(main, Q_OFF + c2 * 2 * LANES, 2 * LANES)
        for c in range(2):
            dst = ACT_Q + (2 * c2 + c) * LANES
            act_ref[:, dst:dst + LANES] = (
                rope(q[:, c * LANES:(c + 1) * LANES]) * (HEAD_DIM ** -0.5 * LOG2E)).astype(BF16)
    kv = proj(main, K_OFF, 2 * KV_WIDTH)
    kT_ref[...] = rope(kv[:, :KV_WIDTH]).T.astype(BF16)
    v0, v1 = _dup_heads(kv[:, KV_WIDTH:])
    v_ref[:, :LANES] = v0.astype(BF16)
    v_ref[:, LANES:] = v1.astype(BF16)
    for c2 in range(ATTN_WIDTH // (2 * LANES)):
        lo = c2 * 2 * LANES
        act_ref[:, ACT_SAG + lo:ACT_SAG + lo + 2 * LANES] = _silu(proj(main, AG_OFF + lo, 2 * LANES)).astype(BF16)

    groups = CONV_ROWS // SUBLANES
    for r0 in range(0, tm, CONV_ROWS):
        slabs = []
        for cb in range(CONV_WIDTH // CONV_COLS):
            ch = slice(cb * CONV_COLS, (cb + 1) * CONV_COLS)
            acc = jnp.zeros((groups, SUBLANES, CONV_COLS), F32) + conv_b_ref[:, ch]
            for t in range(CONV_SIZE):
                top = r0 + U_HALO - CONV_HALF + t
                rows = jnp.concatenate(
                    [ubuf_ref[c, top:top + CONV_ROWS, :]
                     for c in range(cb * CONV_COLS // LANES, (cb + 1) * CONV_COLS // LANES)], axis=1)
                acc = acc + rows.reshape(groups, SUBLANES, CONV_COLS) * conv_w_ref[t, :, ch][None]
            slabs.append(acc.reshape(CONV_ROWS, CONV_COLS))
        cn = _silu(_norm_rows(jnp.concatenate(slabs, axis=1)) * cln_g_ref[...] + cln_b_ref[...])
        act_ref[r0:r0 + CONV_ROWS, ACT_C:ACT_C + CONV_WIDTH] = (
            cn * scg_ref[r0:r0 + CONV_ROWS, :]).astype(BF16)


def _in_proj(x, shift, scale, cos_t, sin_t, w_in, conv_w, conv_b, cln_g, cln_b):
    b, n, d = x.shape
    tm = TILE_ROWS
    hpt = tm // U_HALO
    n_h = n // U_HALO
    const2 = lambda bi, i: (0, 0)
    return pl.pallas_call(
        _in_proj_body,
        grid=(b, n // tm),
        in_specs=[
            pl.BlockSpec((None, tm, d), lambda bi, i: (bi, i, 0)),
            pl.BlockSpec((None, U_HALO, d), lambda bi, i: (bi, jnp.maximum(i * hpt - 1, 0), 0)),
            pl.BlockSpec((None, U_HALO, d), lambda bi, i: (bi, jnp.minimum((i + 1) * hpt, n_h - 1), 0)),
            pl.BlockSpec((None, 1, d), lambda bi, i: (bi, 0, 0)),
            pl.BlockSpec((None, 1, d), lambda bi, i: (bi, 0, 0)),
            pl.BlockSpec((tm, LANES), lambda bi, i: (i, 0)),
            pl.BlockSpec((tm, LANES), lambda bi, i: (i, 0)),
            pl.BlockSpec((d, IN_WIDTH), const2),
            pl.BlockSpec((CONV_SIZE, SUBLANES, CONV_WIDTH), lambda bi, i: (0, 0, 0)),
            pl.BlockSpec((1, CONV_WIDTH), const2),
            pl.BlockSpec((1, CONV_WIDTH), const2),
            pl.BlockSpec((1, CONV_WIDTH), const2),
        ],
        out_specs=[
            pl.BlockSpec((None, tm, ACT_WIDTH), lambda bi, i: (bi, i, 0)),
            pl.BlockSpec((None, KV_WIDTH, tm), lambda bi, i: (bi, 0, i)),
            pl.BlockSpec((None, tm, 2 * LANES), lambda bi, i: (bi, i, 0)),
        ],
        out_shape=[
            jax.ShapeDtypeStruct((b, n, ACT_WIDTH), BF16),
            jax.ShapeDtypeStruct((b, KV_WIDTH, n), BF16),
            jax.ShapeDtypeStruct((b, n, 2 * LANES), BF16),
        ],
        scratch_shapes=[
            pltpu.VMEM((tm + 2 * U_HALO, d), BF16),
            pltpu.VMEM((CONV_WIDTH // LANES, tm + 2 * U_HALO, LANES), F32),
            pltpu.VMEM((tm, CONV_WIDTH), F32),
        ],
        compiler_params=pltpu.CompilerParams(
            dimension_semantics=("arbitrary", "arbitrary"), vmem_limit_bytes=VMEM_LIMIT),
        name="in_proj",
    )(x, x, x, shift, scale, cos_t, sin_t, w_in, conv_w, conv_b, cln_g, cln_b)


def _mixer_body(sink_ref, act_ref,
                kT_ref, kT_prev_ref, kT_next_ref, v_ref, v_prev_ref, v_next_ref,
                kcT_ref, vc_ref, band_ref, w_out_ref,
                x_ref, gate_ref, pln_g_ref, pln_b_ref,
                o_ref,
                v4_ref, kcT4_ref, vc4_ref, mixed_ref, p_ref, esink_ref, kT4_ref, edge_ref):
    tq = x_ref.shape[0]
    i = pl.program_id(1)
    is_first = i == 0
    is_last = i == pl.num_programs(1) - 1
    n_qblk = tq // QBLK

    n_ctx_tiles = kcT_ref.shape[-1] // LANES
    n_loc_tiles = (QBLK + 2 * WINDOW) // LANES
    loc_cols = n_loc_tiles * LANES
    n_tiles = n_loc_tiles + n_ctx_tiles

    @pl.when(is_first)
    def _():
        for h in range(N_KV_HEADS):
            for g in range(GQA_GROUP):
                kcT4_ref[h, g * HEAD_DIM:(g + 1) * HEAD_DIM, :] = kcT_ref[h * HEAD_DIM:(h + 1) * HEAD_DIM, :]
            vc4_ref[h, :, 0:LANES] = vc_ref[:, h * LANES:(h + 1) * LANES]
            vc4_ref[h, :, LANES:] = jnp.ones((vc4_ref.shape[1], LANES), BF16)
            v4_ref[h, :, LANES:] = jnp.ones((v4_ref.shape[1], LANES), BF16)

    for h in range(N_KV_HEADS):
        lanes = slice(h * LANES, (h + 1) * LANES)
        v4_ref[h, 0:WINDOW, 0:LANES] = v_prev_ref[:, lanes]
        v4_ref[h, WINDOW:WINDOW + tq, 0:LANES] = v_ref[:, lanes]
        v4_ref[h, WINDOW + tq:, 0:LANES] = v_next_ref[:, lanes]

    def head_lanes(g):
        lane = lax.broadcasted_iota(jnp.int32, (1, GQA_GROUP * HEAD_DIM), 1)
        return (lane >= g * HEAD_DIM) & (lane < (g + 1) * HEAD_DIM)

    for h in range(N_KV_HEADS):
        rows = slice(h * HEAD_DIM, (h + 1) * HEAD_DIM)
        for g in range(GQA_GROUP):
            dst = slice(g * HEAD_DIM, (g + 1) * HEAD_DIM)
            kT4_ref[h, dst, 0:WINDOW] = kT_prev_ref[rows, :]
            kT4_ref[h, dst, WINDOW:WINDOW + tq] = kT_ref[rows, :]
            kT4_ref[h, dst, WINDOW + tq:] = kT_next_ref[rows, :]
    edge_ref[0] = band_ref[:, 0:WINDOW] + jnp.where(is_first, NEG_INF, 0.0)
    edge_ref[1] = band_ref[:, QBLK + WINDOW:] + jnp.where(is_last, NEG_INF, 0.0)

    def mask_lo(j, rr):
        if j == 0:
            return edge_ref[0, rr:rr + SM_ROWS, :]
        return band_ref[rr:rr + SM_ROWS, 0:WINDOW]

    def mask_hi(j, rr):
        if j == n_qblk - 1:
            return edge_ref[1, rr:rr + SM_ROWS, :]
        return band_ref[rr:rr + SM_ROWS, QBLK + WINDOW:]

    mixed_ref[:, ATTN_WIDTH:] = act_ref[:, ACT_C:ACT_C + CONV_WIDTH]

    for j in range(n_qblk):
        q0 = j * QBLK
        for h in range(N_KV_HEADS):
            hl = slice(h * GQA_GROUP * HEAD_DIM, (h + 1) * GQA_GROUP * HEAD_DIM)
            qb = act_ref[q0:q0 + QBLK, ACT_Q + hl.start:ACT_Q + hl.stop]
            lhs = jnp.concatenate(
                [jnp.where(head_lanes(g), qb, jnp.zeros_like(qb)) for g in range(GQA_GROUP)], axis=0)
            s_loc = jnp.dot(lhs, kT4_ref[h, :, q0:q0 + loc_cols], preferred_element_type=F32)
            s_ctx = jnp.dot(lhs, kcT4_ref[h], preferred_element_type=F32)
            for rc in range(0, GQA_GROUP * QBLK, SM_ROWS):
                g, rr = divmod(rc, QBLK)
                sink = sink_ref[h * GQA_GROUP + g] * LOG2E
                tiles = [s_loc[rc:rc + SM_ROWS, t * LANES:(t + 1) * LANES] for t in range(n_loc_tiles)]
                tiles += [s_ctx[rc:rc + SM_ROWS, t * LANES:(t + 1) * LANES] for t in range(n_ctx_tiles)]
                tiles[0] = tiles[0] + mask_lo(j, rr)
                tiles[n_loc_tiles - 1] = tiles[n_loc_tiles - 1] + mask_hi(j, rr)
                m = jnp.max(functools.reduce(jnp.maximum, tiles), axis=-1, keepdims=True)
                m = jnp.maximum(m, sink)
                esink_ref[h, rc:rc + SM_ROWS, :] = jnp.exp2(sink - m)
                for t in range(n_tiles):
                    p_ref[h, rc:rc + SM_ROWS, t * LANES:(t + 1) * LANES] = jnp.exp2(tiles[t] - m).astype(BF16)
            r = (jnp.dot(p_ref[h, :, 0:loc_cols], v4_ref[h, q0:q0 + loc_cols, :], preferred_element_type=F32)
                 + jnp.dot(p_ref[h, :, loc_cols:], vc4_ref[h], preferred_element_type=F32))
            out = r[:, 0:LANES] * (1.0 / (r[:, LANES:] + esink_ref[h]))
            first_of_pair = lax.broadcasted_iota(jnp.int32, (1, LANES), 1) < HEAD_DIM
            attn = jnp.concatenate(
                [jnp.where(first_of_pair, out[(2 * k) * QBLK:(2 * k + 1) * QBLK],
                           out[(2 * k + 1) * QBLK:(2 * k + 2) * QBLK]) for k in range(GQA_GROUP // 2)], axis=1)
            sag = act_ref[q0:q0 + QBLK, ACT_SAG + hl.start:ACT_SAG + hl.stop].astype(F32)
            mixed_ref[q0:q0 + QBLK, hl] = (attn * sag).astype(BF16)

    y = jnp.dot(mixed_ref[...], w_out_ref[...], preferred_element_type=F32)
    gate = gate_ref[...] * (1.0 / ALPHA)
    for r0 in range(0, tq, QBLK):
        rows = slice(r0, r0 + QBLK)
        z = x_ref[rows, :] + gate * y[rows]
        o_ref[rows, :] = _norm_rows(z, LN_EPS / ALPHA ** 2) * pln_g_ref[...] + pln_b_ref[...]


def _mixer(sink, act, kT, vv, kcT, vc, band, w_out, x, gate, pln_g, pln_b):
    b, n, d = x.shape
    tq = MIXER_ROWS
    m = kcT.shape[-1]
    qpt = tq // QBLK
    n_q = n // QBLK
    const2 = lambda bi, i: (0, 0)
    return pl.pallas_call(
        _mixer_body,
        grid=(b, n // tq),
        in_specs=[
            pl.BlockSpec(memory_space=pltpu.SMEM),
            pl.BlockSpec((None, tq, ACT_WIDTH), lambda bi, i: (bi, i, 0)),
            pl.BlockSpec((None, KV_WIDTH, tq), lambda bi, i: (bi, 0, i)),
            pl.BlockSpec((None, KV_WIDTH, WINDOW), lambda bi, i: (bi, 0, jnp.maximum(i * qpt - 1, 0))),
            pl.BlockSpec((None, KV_WIDTH, WINDOW), lambda bi, i: (bi, 0, jnp.minimum((i + 1) * qpt, n_q - 1))),
            pl.BlockSpec((None, tq, 2 * LANES), lambda bi, i: (bi, i, 0)),
            pl.BlockSpec((None, WINDOW, 2 * LANES), lambda bi, i: (bi, jnp.maximum(i * qpt - 1, 0), 0)),
            pl.BlockSpec((None, WINDOW, 2 * LANES), lambda bi, i: (bi, jnp.minimum((i + 1) * qpt, n_q - 1), 0)),
            pl.BlockSpec((None, KV_WIDTH, m), lambda bi, i: (bi, 0, 0)),
            pl.BlockSpec((None, m, 2 * LANES), lambda bi, i: (bi, 0, 0)),
            pl.BlockSpec((QBLK, QBLK + 2 * WINDOW), const2),
            pl.BlockSpec((d, d), const2),
            pl.BlockSpec((None, tq, d), lambda bi, i: (bi, i, 0)),
            pl.BlockSpec((None, 1, d), lambda bi, i: (bi, 0, 0)),
            pl.BlockSpec((1, d), const2),
            pl.BlockSpec((1, d), const2),
        ],
        out_specs=pl.BlockSpec((None, tq, d), lambda bi, i: (bi, i, 0)),
        out_shape=jax.ShapeDtypeStruct((b, n, d), F32),
        scratch_shapes=[
            pltpu.VMEM((N_KV_HEADS, tq + 2 * WINDOW, 2 * LANES), BF16),
            pltpu.VMEM((N_KV_HEADS, GQA_GROUP * HEAD_DIM, m), BF16),
            pltpu.VMEM((N_KV_HEADS, m, 2 * LANES), BF16),
            pltpu.VMEM((tq, d), BF16),
            pltpu.VMEM((N_KV_HEADS, GQA_GROUP * QBLK, QBLK + 2 * WINDOW + m), BF16),
            pltpu.VMEM((N_KV_HEADS, GQA_GROUP * QBLK, 1), F32),
            pltpu.VMEM((N_KV_HEADS, GQA_GROUP * HEAD_DIM, tq + 2 * WINDOW), BF16),
            pltpu.VMEM((2, QBLK, WINDOW), F32),
        ],
        compiler_params=pltpu.CompilerParams(
            dimension_semantics=("arbitrary", "arbitrary"), vmem_limit_bytes=VMEM_LIMIT),
        name="mixer",
    )(sink, act, kT, kT, kT, vv, vv, vv, kcT, vc, band, w_out, x, gate, pln_g, pln_b)


def _rope_tables(n):
    rows = n // GRID_W
    row = jnp.repeat(jnp.arange(rows, dtype=F32), GRID_W)
    colp = jnp.tile(jnp.arange(GRID_W, dtype=F32), rows)
    inv_freq = ROPE_BASE ** (-jnp.arange(0, ROT_AXIS_DIM, 2, dtype=F32) / ROT_AXIS_DIM)
    ang_row = row[:, None] * inv_freq
    ang_col = colp[:, None] * inv_freq
    cos_h = jnp.concatenate([jnp.cos(ang_row), jnp.cos(ang_row), jnp.cos(ang_col), jnp.cos(ang_col)], axis=-1)
    sin_h = jnp.concatenate([-jnp.sin(ang_row), jnp.sin(ang_row), -jnp.sin(ang_col), jnp.sin(ang_col)], axis=-1)
    reps = LANES // HEAD_DIM
    return jnp.tile(cos_h, (1, reps)), jnp.tile(sin_h, (1, reps))


def _band_bias():
    rel = jnp.arange(QBLK + 2 * WINDOW)[None, :] - jnp.arange(QBLK)[:, None]
    return jnp.where((rel >= 0) & (rel <= 2 * WINDOW), 0.0, NEG_INF).astype(F32)


def kernel(x, c, ctx, c_ctx, w_ada, b_ada, w_in, attn_sink, conv_w, conv_b, conv_ln_g, conv_ln_b,
           w_out, post_ln_g, post_ln_b):
    assert w_ada.shape[0] == DEPTH
    b, n, d = x.shape
    cond = jnp.zeros((ADA_ROWS, d), F32).at[:b].set(c).at[b].set(c_ctx)
    mods = _adaln(cond, w_ada[0], b_ada[0][None, :])
    shift, scale, gate = (mods[:b, k * d:(k + 1) * d][:, None, :] for k in range(3))
    shift_c, scale_c = (mods[b:b + 1, k * d:(k + 1) * d] for k in range(2))

    w_in_b = w_in[0].astype(BF16)
    kcT, vc = _ctx_kv(ctx, shift_c, scale_c, w_in_b[:, K_OFF:AG_OFF])
    cos_t, sin_t = _rope_tables(n)
    conv_w_rep = jnp.broadcast_to(conv_w[0][:, None, :], (CONV_SIZE, SUBLANES, CONV_WIDTH))
    act, kT, vv = _in_proj(x, shift, scale, cos_t, sin_t, w_in_b,
                           conv_w_rep, conv_b[0][None, :], conv_ln_g[0][None, :], conv_ln_b[0][None, :])
    return _mixer(attn_sink[0], act, kT, vv, kcT, vc, _band_bias(),
                  w_out[0].astype(BF16), x, gate, post_ln_g[0][None, :], post_ln_b[0][None, :])
```
